```python
import jax, jax.numpy as jnp
from jax import lax
import numpy as np

D_MODEL = 1024
BATCH = 8
SEQ = 2048
DEPTH = 4

N_A_LAYERS = DEPTH // 2
N_B_LAYERS = DEPTH - N_A_LAYERS
N_DENSE = (DEPTH + 1) // 2
N_MOE = DEPTH // 2

HG_HEADS = 8
HG_KEY_DIM = 128
HG_VAL_DIM = D_MODEL // HG_HEADS
HG_Q = HG_HEADS * HG_KEY_DIM
HG_V = HG_HEADS * HG_VAL_DIM
HG_CHUNK = 64
LB_FLOOR = 1e-30

FOX_HEADS = 16
FOX_HEAD_DIM = 64
FOX_DIM = FOX_HEADS * FOX_HEAD_DIM
Q_BLOCK = 128
FGATE_BIAS_MEAN = 3.0
MASK_VALUE = -1e30

FFN_DIM = 2816
N_EXPERTS = 8
TOP_K = 2
EXPERT_DIM = 3584

ALPHA = (2 * DEPTH) ** 0.25
BETA = (8 * DEPTH) ** -0.25
LN_EPS = 1e-5
RMS_EPS = 1e-6

kernel_name = "yoco_hgrn2_fox_moe_deepnorm"


def layer_norm(x, g, b):
    xf = x.astype(jnp.float32)
    mu = jnp.mean(xf, axis=-1, keepdims=True)
    var = jnp.mean(jnp.square(xf - mu), axis=-1, keepdims=True)
    y = (xf - mu) * lax.rsqrt(var + LN_EPS) * g.astype(jnp.float32) + b.astype(jnp.float32)
    return y.astype(x.dtype)


def deepnorm_residual(x, y, g, b):
    return layer_norm(ALPHA * x + y, g, b)


def hgrn2_lower_bounds(lb_logits):
    p = jax.nn.softmax(lb_logits.astype(jnp.float32), axis=0)
    return jnp.cumsum(p, axis=0) - p[0]


def hgrn2_mixer(x, w_in, lb, norm_w, w_out):
    bsz, s, _ = x.shape
    proj = x @ w_in
    q, f, i, g = jnp.split(proj, [HG_Q, 2 * HG_Q, 2 * HG_Q + HG_V], axis=-1)
    z = f.astype(jnp.float32)
    log_lb = jnp.log(jnp.maximum(lb, LB_FLOOR))
    log_f = jnp.logaddexp(jax.nn.log_sigmoid(z), log_lb + jax.nn.log_sigmoid(-z))
    k = (1.0 - lb) * jax.nn.sigmoid(-z)
    q = jax.nn.silu(q.astype(jnp.float32))
    v = i.astype(jnp.float32)
    n_chunks = s // HG_CHUNK

    def to_chunks(t, d):
        return t.reshape(bsz, n_chunks, HG_CHUNK, HG_HEADS, d).transpose(1, 0, 3, 2, 4)

    qc, kc, vc, gc = (to_chunks(q, HG_KEY_DIM), to_chunks(k, HG_KEY_DIM),
                      to_chunks(v, HG_VAL_DIM), to_chunks(log_f, HG_KEY_DIM))
    causal = jnp.tril(jnp.ones((HG_CHUNK, HG_CHUNK), dtype=bool))[:, :, None]

    def chunk_step(state, inp):
        q_c, k_c, v_c, g_c = inp
        b = jnp.cumsum(g_c, axis=2)
        o_inter = jnp.einsum('bhtd,bhde->bhte', q_c * jnp.exp(b), state)
        diff = b[:, :, :, None, :] - b[:, :, None, :, :]
        decay = jnp.where(causal, jnp.exp(jnp.where(causal, diff, 0.0)), 0.0)
        scores = jnp.einsum('bhtsd,bhsd->bhts', q_c[:, :, :, None, :] * decay, k_c)
        o_intra = jnp.einsum('bhts,bhse->bhte', scores, v_c)
        b_last = b[:, :, -1:, :]
        new_state = (jnp.exp(b_last[:, :, 0, :])[..., None] * state
                     + jnp.einsum('bhsd,bhse->bhde', k_c * jnp.exp(b_last - b), v_c))
        return new_state, o_inter + o_intra

    state0 = jnp.zeros((bsz, HG_HEADS, HG_KEY_DIM, HG_VAL_DIM), jnp.float32)
    _, o = lax.scan(chunk_step, state0, (qc, kc, vc, gc))
    o = o.transpose(1, 0, 3, 2, 4).reshape(bsz, s, HG_HEADS, HG_VAL_DIM)
    o = o * lax.rsqrt(jnp.mean(jnp.square(o), axis=-1, keepdims=True) + RMS_EPS)
    o = o * norm_w.astype(jnp.float32) * jax.nn.silu(g.astype(jnp.float32)).reshape(bsz, s, HG_HEADS, HG_VAL_DIM)
    return o.reshape(bsz, s, HG_V).astype(x.dtype) @ w_out


def shared_kv(x, w_kv, b_f):
    bsz, s, _ = x.shape
    kv = x @ w_kv
    k, v, f = jnp.split(kv, [FOX_DIM, 2 * FOX_DIM], axis=-1)
    k = k.reshape(bsz, s, FOX_HEADS, FOX_HEAD_DIM).transpose(0, 2, 1, 3)
    v = v.reshape(bsz, s, FOX_HEADS, FOX_HEAD_DIM).transpose(0, 2, 1, 3)
    log_f = jax.nn.log_sigmoid(f.astype(jnp.float32) + b_f.astype(jnp.float32))
    cum_log_f = jnp.cumsum(log_f, axis=1).transpose(0, 2, 1)
    return k, v, cum_log_f


def fox_mixer(x, w_qg, k, v, cum_log_f, w_out):
    bsz, s, _ = x.shape
    q, gate = jnp.split(x @ w_qg, [FOX_DIM], axis=-1)
    q = q.reshape(bsz, s, FOX_HEADS, FOX_HEAD_DIM).transpose(0, 2, 1, 3)
    scale = FOX_HEAD_DIM ** -0.5
    outs = []
    for blk in range(s // Q_BLOCK):
        qs, qe = blk * Q_BLOCK, (blk + 1) * Q_BLOCK
        logits = jnp.einsum('bhqd,bhkd->bhqk', q[:, :, qs:qe], k[:, :, :qe]).astype(jnp.float32) * scale
        logits = logits + cum_log_f[:, :, qs:qe, None] - cum_log_f[:, :, None, :qe]
        mask = jnp.arange(qe)[None, :] <= (qs + jnp.arange(Q_BLOCK))[:, None]
        probs = jax.nn.softmax(jnp.where(mask, logits, MASK_VALUE), axis=-1)
        outs.append(jnp.einsum('bhqk,bhkd->bhqd', probs.astype(v.dtype), v[:, :, :qe]))
    o = jnp.concatenate(outs, axis=2).transpose(0, 2, 1, 3).reshape(bsz, s, FOX_DIM)
    return (o * jax.nn.sigmoid(gate)) @ w_out


def swiglu(x, w_gu, w_down):
    a, b = jnp.split(x @ w_gu, 2, axis=-1)
    return (jax.nn.silu(a) * b) @ w_down


def moe_swiglu(x, w_router, w_gu, w_down):
    bsz, s, d = x.shape
    xf = x.reshape(-1, d)
    logits = (xf @ w_router).astype(jnp.float32)
    top_vals, top_idx = lax.top_k(logits, TOP_K)
    top_w = jax.nn.softmax(top_vals, axis=-1)
    combine = jnp.sum(jax.nn.one_hot(top_idx, N_EXPERTS, dtype=jnp.float32) * top_w[..., None], axis=1)
    y = jnp.zeros(xf.shape, jnp.float32)
    for e in range(N_EXPERTS):
        y = y + combine[:, e:e + 1] * swiglu(xf, w_gu[e], w_down[e]).astype(jnp.float32)
    return y.astype(x.dtype).reshape(bsz, s, d)


def setup_inputs(seed: int = 0) -> dict:
    key = jax.random.key(seed)
    ks = jax.random.split(key, 20)

    def nrm(k, shape, scale):
        return jax.random.normal(k, shape, jnp.float32) * scale

    hg_col_scale = jnp.concatenate([jnp.ones((2 * HG_Q,), jnp.float32), jnp.full((HG_V,), BETA, jnp.float32),
                                    jnp.ones((HG_V,), jnp.float32)])
    kv_col_scale = jnp.concatenate([jnp.ones((FOX_DIM,), jnp.float32), jnp.full((FOX_DIM,), BETA, jnp.float32),
                                    jnp.ones((FOX_HEADS,), jnp.float32)])
    return {
        "x": nrm(ks[0], (BATCH, SEQ, D_MODEL), 1.0),
        "hg_w_in": nrm(ks[1], (N_A_LAYERS, D_MODEL, 2 * HG_Q + 2 * HG_V), D_MODEL ** -0.5) * hg_col_scale,
        "hg_lb_logits": 1.0 + nrm(ks[2], (N_A_LAYERS, HG_Q), 0.1),
        "hg_norm_w": 1.0 + nrm(ks[3], (N_A_LAYERS, HG_VAL_DIM), 0.1),
        "hg_w_out": nrm(ks[4], (N_A_LAYERS, HG_V, D_MODEL), HG_V ** -0.5 * BETA),
        "kv_w": nrm(ks[5], (D_MODEL, 2 * FOX_DIM + FOX_HEADS), D_MODEL ** -0.5) * kv_col_scale,
        "kv_fgate_b": FGATE_BIAS_MEAN + nrm(ks[6], (FOX_HEADS,), 0.5),
        "fox_w_qg": nrm(ks[7], (N_B_LAYERS, D_MODEL, 2 * FOX_DIM), D_MODEL ** -0.5),
        "fox_w_out": nrm(ks[8], (N_B_LAYERS, FOX_DIM, D_MODEL), FOX_DIM ** -0.5 * BETA),
        "ln_mix_g": 1.0 + nrm(ks[9], (DEPTH, D_MODEL), 0.05),
        "ln_mix_b": nrm(ks[10], (DEPTH, D_MODEL), 0.02),
        "ln_ffn_g": 1.0 + nrm(ks[11], (DEPTH, D_MODEL), 0.05),
        "ln_ffn_b": nrm(ks[12], (DEPTH, D_MODEL), 0.02),
        "ffn_w_gu": nrm(ks[13], (N_DENSE, D_MODEL, 2 * FFN_DIM), D_MODEL ** -0.5),
        "ffn_w_down": nrm(ks[14], (N_DENSE, FFN_DIM, D_MODEL), FFN_DIM ** -0.5 * BETA),
        "moe_w_router": nrm(ks[15], (N_MOE, D_MODEL, N_EXPERTS), D_MODEL ** -0.5),
        "moe_w_gu": nrm(ks[16], (N_MOE, N_EXPERTS, D_MODEL, 2 * EXPERT_DIM), D_MODEL ** -0.5),
        "moe_w_down": nrm(ks[17], (N_MOE, N_EXPERTS, EXPERT_DIM, D_MODEL), EXPERT_DIM ** -0.5 * BETA),
    }


def reference(x, hg_w_in, hg_lb_logits, hg_norm_w, hg_w_out, kv_w, kv_fgate_b, fox_w_qg, fox_w_out,
              ln_mix_g, ln_mix_b, ln_ffn_g, ln_ffn_b, ffn_w_gu, ffn_w_down,
              moe_w_router, moe_w_gu, moe_w_down):
    lbs = hgrn2_lower_bounds(hg_lb_logits)
    k_sh = v_sh = c_sh = None
    for layer in range(DEPTH):
        if layer < N_A_LAYERS:
            a = layer
            y = hgrn2_mixer(x, hg_w_in[a], lbs[a], hg_norm_w[a], hg_w_out[a])
        else:
            bi = layer - N_A_LAYERS
            y = fox_mixer(x, fox_w_qg[bi], k_sh, v_sh, c_sh, fox_w_out[bi])
        x = deepnorm_residual(x, y, ln_mix_g[layer], ln_mix_b[layer])
        if layer % 2 == 0:
            y = swiglu(x, ffn_w_gu[layer // 2], ffn_w_down[layer // 2])
        else:
            y = moe_swiglu(x, moe_w_router[layer // 2], moe_w_gu[layer // 2], moe_w_down[layer // 2])
        x = deepnorm_residual(x, y, ln_ffn_g[layer], ln_ffn_b[layer])
        if layer == N_A_LAYERS - 1:
            k_sh, v_sh, c_sh = shared_kv(x, kv_w, kv_fgate_b)
    return x
```

```python
import functools

import jax
import jax.numpy as jnp
from jax import lax
from jax.experimental import pallas as pl
from jax.experimental.pallas import tpu as pltpu

F32 = jnp.float32
BF16 = jnp.bfloat16

HG_HEADS = 8
HG_CHUNK = 64
HG_SUB = 16
LB_FLOOR = 1e-30
FOX_HEADS = 16
FOX_HEAD_DIM = 64
N_EXPERTS = 8
TOP_K = 2
LN_EPS = 1e-5
RMS_EPS = 1e-6

LANES = 128
VMEM_LIMIT_BYTES = 56 * 1024 * 1024

NEG_BIG = -1e30


def _params(*sem):
    return pltpu.CompilerParams(dimension_semantics=sem, vmem_limit_bytes=VMEM_LIMIT_BYTES)


def _layer_norm(h, g, b):
    mu = jnp.mean(h, axis=-1, keepdims=True)
    d = h - mu
    var = jnp.mean(d * d, axis=-1, keepdims=True)
    return d * lax.rsqrt(var + LN_EPS) * g + b


def _silu(x):
    return x * jax.nn.sigmoid(x)


def _log_sigmoid(z):
    return jnp.minimum(z, 0.0) - jnp.log1p(jnp.exp(-jnp.abs(z)))


def _split3(x):
    hi = x.astype(BF16)
    r1 = x - hi.astype(F32)
    mid = r1.astype(BF16)
    lo = (r1 - mid.astype(F32)).astype(BF16)
    return hi, mid, lo


def _tri(n):
    r = lax.broadcasted_iota(jnp.int32, (n, n), 0)
    c = lax.broadcasted_iota(jnp.int32, (n, n), 1)
    return (r >= c).astype(BF16)


def _cumsum_rows(x, tri):
    hi, mid, lo = _split3(x)
    return (jnp.dot(tri, hi, preferred_element_type=F32)
            + jnp.dot(tri, mid, preferred_element_type=F32)
            + jnp.dot(tri, lo, preferred_element_type=F32))


def _hgrn_proj_kernel(x_ref, w_ref, lb_ref, q_ref, k_ref, v_ref, lf_ref, g_ref):
    j = pl.program_id(1)
    acc = jnp.dot(x_ref[...].astype(BF16), w_ref[...], preferred_element_type=F32)

    @pl.when(j == 0)
    def _():
        q_ref[...] = _silu(acc)

    @pl.when(j == 1)
    def _():
        lb = lb_ref[...]
        log_lb = jnp.log(jnp.maximum(lb, LB_FLOOR))
        a = _log_sigmoid(acc)
        b = log_lb + _log_sigmoid(-acc)
        lf_ref[...] = jnp.maximum(a, b) + jnp.log1p(jnp.exp(-jnp.abs(a - b)))
        k_ref[...] = (1.0 - lb) * jax.nn.sigmoid(-acc)

    @pl.when(j == 2)
    def _():
        v_ref[...] = acc

    @pl.when(j == 3)
    def _():
        g_ref[...] = _silu(acc).astype(g_ref.dtype)


def hgrn_proj(x2, w_in, lb, *, tm=512):
    m, d = x2.shape
    n = w_in.shape[1] // 4
    row = lambda i, j: (i, 0)
    out_sd = jax.ShapeDtypeStruct((m, n), F32)
    return pl.pallas_call(
        _hgrn_proj_kernel,
        grid=(m // tm, 4),
        in_specs=[pl.BlockSpec((tm, d), row),
                  pl.BlockSpec((d, n), lambda i, j: (0, j)),
                  pl.BlockSpec((1, n), lambda i, j: (0, 0))],
        out_specs=[pl.BlockSpec((tm, n), row)] * 5,
        out_shape=[out_sd, out_sd, out_sd, out_sd, jax.ShapeDtypeStruct((m, n), BF16)],
        compiler_params=_params("parallel", "arbitrary"),
        name="hgrn_proj",
    )(x2, w_in, lb)


def _hgrn_chunk(q, k, v, lf, st, tri):
    c = q.shape[0]
    b = _cumsum_rows(lf, tri)
    nt = (((1,), (1,)), ((), ()))
    o = lax.dot_general((q * jnp.exp(b)).astype(BF16), st.astype(BF16), nt,
                        preferred_element_type=F32)
    b_last = b[c - 1:c]
    kd = k * jnp.exp(b_last - b)
    upd = lax.dot_general(v.astype(BF16), kd.astype(BF16), (((0,), (0,)), ((), ())),
                          preferred_element_type=F32)
    st_new = st * jnp.exp(b_last) + upd

    row = lax.broadcasted_iota(jnp.int32, (HG_SUB, 1), 0)
    outs = []
    for i in range(c // HG_SUB):
        lo, hi = i * HG_SUB, (i + 1) * HG_SUB
        bi, qi, ki, vi = b[lo:hi], q[lo:hi], k[lo:hi], v[lo:hi]
        oi = o[lo:hi]
        if i > 0:
            r = b[lo - 1:lo]
            qd = qi * jnp.exp(bi - r)
            kp = k[:lo] * jnp.exp(r - b[:lo])
            p = lax.dot_general(qd.astype(BF16), kp.astype(BF16), nt, preferred_element_type=F32)
            oi = oi + jnp.dot(p.astype(BF16), v[:lo].astype(BF16), preferred_element_type=F32)
        for s in range(HG_SUB):
            e = jnp.exp(jnp.minimum(bi - bi[s:s + 1], 0.0))
            a = jnp.sum(qi * e * ki[s:s + 1], axis=-1, keepdims=True)
            a = jnp.where(row >= s, a, 0.0)
            oi = oi + a * vi[s:s + 1]
        outs.append(oi)
    return jnp.concatenate(outs, axis=0), st_new


def _hgrn_rec_kernel(q_ref, k_ref, v_ref, lf_ref, g_ref, nw_ref, o_ref, st_ref, *, n_chunks):
    @pl.when(pl.program_id(2) == 0)
    def _():
        st_ref[...] = jnp.zeros_like(st_ref)

    tri = _tri(HG_CHUNK)
    nw = nw_ref[...]

    def body(ci, carry):
        off = pl.multiple_of(ci * HG_CHUNK, HG_CHUNK)
        sl = pl.ds(off, HG_CHUNK)
        o, st_new = _hgrn_chunk(q_ref[0, sl, :], k_ref[0, sl, :], v_ref[0, sl, :], lf_ref[0, sl, :],
                                st_ref[...], tri)
        st_ref[...] = st_new
        o = o * lax.rsqrt(jnp.mean(o * o, axis=-1, keepdims=True) + RMS_EPS)
        o = o * nw * g_ref[0, sl, :].astype(F32)
        o_ref[0, sl, :] = o.astype(o_ref.dtype)
        return carry

    lax.fori_loop(0, n_chunks, body, 0)


def hgrn_recurrence(q, k, v, lf, g, norm_w, *, sblk=512):
    bsz, s, hd = q.shape
    dh = hd // HG_HEADS
    blk = pl.BlockSpec((1, sblk, dh), lambda b, h, t: (b, t, h))
    return pl.pallas_call(
        functools.partial(_hgrn_rec_kernel, n_chunks=sblk // HG_CHUNK),
        grid=(bsz, HG_HEADS, s // sblk),
        in_specs=[blk, blk, blk, blk, blk, pl.BlockSpec((1, dh), lambda b, h, t: (0, 0))],
        out_specs=blk,
        out_shape=jax.ShapeDtypeStruct((bsz, s, hd), BF16),
        scratch_shapes=[pltpu.VMEM((dh, dh), F32)],
        compiler_params=_params("parallel", "parallel", "arbitrary"),
        name="hgrn_recurrence",
    )(q, k, v, lf, g, norm_w)


def _mm_ln_kernel(h_ref, w_ref, x_ref, g_ref, b_ref, o_ref, *, alpha):
    y = jnp.dot(h_ref[...], w_ref[...], preferred_element_type=F32)
    o_ref[...] = _layer_norm(alpha * x_ref[...] + y, g_ref[...], b_ref[...])


def mm_ln(h, w, x2, g, b, *, alpha, tm=512):
    m, kdim = h.shape
    d = w.shape[1]
    row = lambda i: (i, 0)
    fixed = lambda i: (0, 0)
    return pl.pallas_call(
        functools.partial(_mm_ln_kernel, alpha=alpha),
        grid=(m // tm,),
        in_specs=[pl.BlockSpec((tm, kdim), row), pl.BlockSpec((kdim, d), fixed),
                  pl.BlockSpec((tm, d), row), pl.BlockSpec((1, d), fixed), pl.BlockSpec((1, d), fixed)],
        out_specs=pl.BlockSpec((tm, d), row),
        out_shape=jax.ShapeDtypeStruct((m, d), F32),
        compiler_params=_params("parallel"),
        name="mm_ln",
    )(h, w, x2, g, b)


def _ffn_ln_kernel(x_ref, wg_ref, wu_ref, wd_ref, g_ref, b_ref, o_ref, acc_ref, *, alpha):
    c = pl.program_id(1)
    xb = x_ref[...].astype(BF16)
    a = jnp.dot(xb, wg_ref[...], preferred_element_type=F32)
    u = jnp.dot(xb, wu_ref[...], preferred_element_type=F32)
    part = jnp.dot((_silu(a) * u).astype(BF16), wd_ref[...], preferred_element_type=F32)

    @pl.when(c == 0)
    def _():
        acc_ref[...] = part

    @pl.when(c > 0)
    def _():
        acc_ref[...] += part

    @pl.when(c == pl.num_programs(1) - 1)
    def _():
        o_ref[...] = _layer_norm(alpha * x_ref[...] + acc_ref[...], g_ref[...], b_ref[...])


def ffn_ln(x2, w_gu, w_down, g, b, *, alpha, tm=512, th=1408):
    m, d = x2.shape
    f = w_down.shape[0]
    nc = f // th
    row = lambda i, c: (i, 0)
    fixed = lambda i, c: (0, 0)
    return pl.pallas_call(
        functools.partial(_ffn_ln_kernel, alpha=alpha),
        grid=(m // tm, nc),
        in_specs=[pl.BlockSpec((tm, d), row),
                  pl.BlockSpec((d, th), lambda i, c: (0, c)),
                  pl.BlockSpec((d, th), lambda i, c: (0, nc + c)),
                  pl.BlockSpec((th, d), lambda i, c: (c, 0)),
                  pl.BlockSpec((1, d), fixed), pl.BlockSpec((1, d), fixed)],
        out_specs=pl.BlockSpec((tm, d), row),
        out_shape=jax.ShapeDtypeStruct((m, d), F32),
        scratch_shapes=[pltpu.VMEM((tm, d), F32)],
        compiler_params=_params("parallel", "arbitrary"),
        name="ffn_ln",
    )(x2, w_gu, w_gu, w_down, g, b)


def _router_kernel(x_ref, wh_ref, wl_ref, o_ref):
    x = x_ref[...]
    xh = x.astype(BF16)
    xl = (x - xh.astype(F32)).astype(BF16)
    wh, wl = wh_ref[...], wl_ref[...]
    logits = (jnp.dot(xh, wh, preferred_element_type=F32)
              + (jnp.dot(xl, wh, preferred_element_type=F32) + jnp.dot(xh, wl, preferred_element_type=F32)))
    lane = lax.broadcasted_iota(jnp.int32, logits.shape, 1)
    logits = jnp.where(lane < N_EXPERTS, logits, -jnp.inf)
    m1 = jnp.max(logits, axis=-1, keepdims=True)
    i1 = jnp.min(jnp.where(logits == m1, lane, LANES), axis=-1, keepdims=True)
    rest = jnp.where(lane == i1, -jnp.inf, logits)
    m2 = jnp.max(rest, axis=-1, keepdims=True)
    i2 = jnp.min(jnp.where(rest == m2, lane, LANES), axis=-1, keepdims=True)
    e2 = jnp.exp(m2 - m1)
    den = 1.0 + e2
    w1 = 1.0 / den
    w2 = e2 / den
    o_ref[...] = jnp.where(lane == 0, i1.astype(F32),
                           jnp.where(lane == 1, i2.astype(F32),
                                     jnp.where(lane == 2, w1, jnp.where(lane == 3, w2, 0.0))))


def router(x2, w_hi, w_lo, *, tm=1024):
    m, d = x2.shape
    row = lambda i: (i, 0)
    fixed = lambda i: (0, 0)
    return pl.pallas_call(
        _router_kernel,
        grid=(m // tm,),
        in_specs=[pl.BlockSpec((tm, d), row), pl.BlockSpec((d, LANES), fixed), pl.BlockSpec((d, LANES), fixed)],
        out_specs=pl.BlockSpec((tm, LANES), row),
        out_shape=jax.ShapeDtypeStruct((m, LANES), F32),
        compiler_params=_params("parallel"),
        name="moe_router",
    )(x2, w_hi, w_lo)


def _row_copy(src_hbm, dst_vmem, sem, src_row, dst_row):
    return pltpu.make_async_copy(src_hbm.at[pl.ds(src_row, 1)], dst_vmem.at[pl.ds(dst_row, 1)], sem)


def _gather_into(idx_ref, base, src_hbm, dst_vmem, sem, n_rows):
    def start(r, c):
        _row_copy(src_hbm, dst_vmem, sem, idx_ref[base + r], r).start()
        return c

    lax.fori_loop(0, n_rows, start, 0, unroll=8)

    def wait(r, c):
        _row_copy(src_hbm, dst_vmem, sem, 0, r).wait()
        return c

    lax.fori_loop(0, n_rows, wait, 0, unroll=8)


def _gather_kernel(idx_ref, src_hbm, o_ref, sem, *, rows):
    _gather_into(idx_ref, pl.program_id(0) * rows, src_hbm, o_ref, sem, rows)


def gather_rows(src, idx, *, rows=256):
    n = idx.shape[0]
    d = src.shape[1]
    return pl.pallas_call(
        functools.partial(_gather_kernel, rows=rows),
        grid_spec=pltpu.PrefetchScalarGridSpec(
            num_scalar_prefetch=1,
            grid=(n // rows,),
            in_specs=[pl.BlockSpec(memory_space=pl.ANY)],
            out_specs=pl.BlockSpec((rows, d), lambda i, idx: (i, 0)),
            scratch_shapes=[pltpu.SemaphoreType.DMA(())]),
        out_shape=jax.ShapeDtypeStruct((n, d), src.dtype),
        compiler_params=_params("arbitrary"),
        name="moe_gather",
    )(idx, src)


def _experts_kernel(te_ref, nu_ref, xs_ref, wg_ref, wu_ref, wd_ref, o_ref, acc_ref):
    r, c = pl.program_id(0), pl.program_id(1)

    @pl.when(r < nu_ref[0])
    def _():
        xb = xs_ref[...].astype(BF16)
        a = jnp.dot(xb, wg_ref[0], preferred_element_type=F32)
        u = jnp.dot(xb, wu_ref[0], preferred_element_type=F32)
        part = jnp.dot((_silu(a) * u).astype(BF16), wd_ref[0], preferred_element_type=F32)

        @pl.when(c == 0)
        def _():
            acc_ref[...] = part

        @pl.when(c > 0)
        def _():
            acc_ref[...] += part

        @pl.when(c == pl.num_programs(1) - 1)
        def _():
            o_ref[...] = acc_ref[...]

    @pl.when(jnp.logical_and(r >= nu_ref[0], c == pl.num_programs(1) - 1))
    def _():
        o_ref[...] = jnp.zeros_like(o_ref)


def experts(xs, w_gu, w_down, tile_expert, n_used, *, tm, th=512):
    n, d = xs.shape
    f = w_down.shape[1]
    nc = f // th
    return pl.pallas_call(
        _experts_kernel,
        grid_spec=pltpu.PrefetchScalarGridSpec(
            num_scalar_prefetch=2,
            grid=(n // tm, nc),
            in_specs=[pl.BlockSpec((tm, d), lambda r, c, te, nu: (r, 0)),
                      pl.BlockSpec((1, d, th), lambda r, c, te, nu: (te[r], 0, c)),
                      pl.BlockSpec((1, d, th), lambda r, c, te, nu: (te[r], 0, nc + c)),
                      pl.BlockSpec((1, th, d), lambda r, c, te, nu: (te[r], c, 0))],
            out_specs=pl.BlockSpec((tm, d), lambda r, c, te, nu: (r, 0)),
            scratch_shapes=[pltpu.VMEM((tm, d), F32)]),
        out_shape=jax.ShapeDtypeStruct((n, d), F32),
        compiler_params=_params("arbitrary", "arbitrary"),
        name="moe_experts",
    )(tile_expert, n_used, xs, w_gu, w_gu, w_down)


def _combine_ln_kernel(s1_ref, s2_ref, ys_hbm, x_ref, rw_ref, g_ref, b_ref, o_ref, a_buf, b_buf, sem, *,
                       alpha, rows):
    base = pl.program_id(0) * rows
    _gather_into(s1_ref, base, ys_hbm, a_buf, sem.at[0], rows)
    _gather_into(s2_ref, base, ys_hbm, b_buf, sem.at[1], rows)
    rw = rw_ref[...]
    y = rw[:, 2:3] * a_buf[...] + rw[:, 3:4] * b_buf[...]
    o_ref[...] = _layer_norm(alpha * x_ref[...] + y, g_ref[...], b_ref[...])


def combine_ln(ys, slot1, slot2, x2, rw, g, b, *, alpha, rows=256):
    m, d = x2.shape
    row = lambda i, s1, s2: (i, 0)
    fixed = lambda i, s1, s2: (0, 0)
    return pl.pallas_call(
        functools.partial(_combine_ln_kernel, alpha=alpha, rows=rows),
        grid_spec=pltpu.PrefetchScalarGridSpec(
            num_scalar_prefetch=2,
            grid=(m // rows,),
            in_specs=[pl.BlockSpec(memory_space=pl.ANY),
                      pl.BlockSpec((rows, d), row), pl.BlockSpec((rows, LANES), row),
                      pl.BlockSpec((1, d), fixed), pl.BlockSpec((1, d), fixed)],
            out_specs=pl.BlockSpec((rows, d), row),
            scratch_shapes=[pltpu.VMEM((rows, d), F32), pltpu.VMEM((rows, d), F32),
                            pltpu.SemaphoreType.DMA((2,))]),
        out_shape=jax.ShapeDtypeStruct((m, d), F32),
        compiler_params=_params("arbitrary"),
        name="moe_combine_ln",
    )(slot1, slot2, ys, x2, rw, g, b)


def moe_ln(x2, w_r_hi, w_r_lo, w_gu, w_down, g, b, *, alpha, tm=512):
    t = x2.shape[0]
    rw = router(x2, w_r_hi, w_r_lo)
    e_idx = rw[:, :TOP_K].astype(jnp.int32)
    flat_e = e_idx.reshape(-1)
    onehot = (flat_e[:, None] == jnp.arange(N_EXPERTS, dtype=jnp.int32)[None, :]).astype(jnp.int32)
    rank = jnp.take_along_axis(jnp.cumsum(onehot, axis=0) - onehot, flat_e[:, None], axis=1)[:, 0]
    counts = jnp.sum(onehot, axis=0)
    padded = ((counts + tm - 1) // tm) * tm
    ends = jnp.cumsum(padded)
    starts = ends - padded
    slot = starts[flat_e] + rank
    n_tiles = (TOP_K * t) // tm + N_EXPERTS
    n_slots = n_tiles * tm
    token = jnp.arange(TOP_K * t, dtype=jnp.int32) // TOP_K
    token_of_slot = jnp.zeros((n_slots,), jnp.int32).at[slot].set(token)
    tile_start = jnp.arange(n_tiles, dtype=jnp.int32) * tm
    tile_expert = jnp.minimum(jnp.sum((tile_start[:, None] >= ends[None, :]).astype(jnp.int32), axis=1),
                              N_EXPERTS - 1).astype(jnp.int32)
    n_used = (ends[-1:] // tm).astype(jnp.int32)

    xs = gather_rows(x2, token_of_slot)
    ys = experts(xs, w_gu, w_down, tile_expert, n_used, tm=tm)
    slot2d = slot.reshape(t, TOP_K).astype(jnp.int32)
    return combine_ln(ys, slot2d[:, 0], slot2d[:, 1], x2, rw, g, b, alpha=alpha)


def _mm_kernel(x_ref, w_ref, bias_ref, o_ref, *, kind, scale):
    acc = jnp.dot(x_ref[...].astype(BF16), w_ref[...], preferred_element_type=F32)
    if kind == "scale":
        acc = acc * scale
    elif kind == "sigmoid":
        acc = jax.nn.sigmoid(acc)
    elif kind == "log_sigmoid_bias":
        acc = _log_sigmoid(acc + bias_ref[...])
    o_ref[...] = acc.astype(o_ref.dtype)


def mm(x2, w, *, kind="none", scale=1.0, bias=None, out_dtype=BF16, tm=1024, tn=None):
    m, d = x2.shape
    n = w.shape[1]
    tn = n if tn is None else tn
    if bias is None:
        bias = jnp.zeros((1, n), F32)
    return pl.pallas_call(
        functools.partial(_mm_kernel, kind=kind, scale=scale),
        grid=(m // tm, n // tn),
        in_specs=[pl.BlockSpec((tm, d), lambda i, j: (i, 0)), pl.BlockSpec((d, tn), lambda i, j: (0, j)),
                  pl.BlockSpec((1, tn), lambda i, j: (0, j))],
        out_specs=pl.BlockSpec((tm, tn), lambda i, j: (i, j)),
        out_shape=jax.ShapeDtypeStruct((m, n), out_dtype),
        compiler_params=_params("parallel", "arbitrary"),
        name="mm_" + kind,
    )(x2, w, bias)


def _seq_cumsum_kernel(x_ref, o_ref, *, blk):
    tri = _tri(blk)
    n = x_ref.shape[1] // blk

    def body(i, carry):
        sl = pl.ds(pl.multiple_of(i * blk, blk), blk)
        c = _cumsum_rows(x_ref[0, sl, :], tri) + carry
        o_ref[0, sl, :] = c
        return c[blk - 1:blk]

    lax.fori_loop(0, n, body, jnp.zeros((1, x_ref.shape[2]), F32))


def seq_cumsum(x, *, blk=256):
    bsz, s, n = x.shape
    spec = pl.BlockSpec((1, s, n), lambda b: (b, 0, 0))
    return pl.pallas_call(
        functools.partial(_seq_cumsum_kernel, blk=blk),
        grid=(bsz,), in_specs=[spec], out_specs=spec,
        out_shape=jax.ShapeDtypeStruct(x.shape, F32),
        compiler_params=_params("parallel"),
        name="seq_cumsum",
    )(x)


def _fox_kernel(q_ref, k_ref, v_ref, cq_ref, ck_ref, gate_ref, o_ref, m_ref, l_ref, acc_ref, *, tq, tk):
    qi = pl.program_id(2)
    dh = FOX_HEAD_DIM
    nt = (((1,), (1,)), ((), ()))
    rowi = lax.broadcasted_iota(jnp.int32, (tq, tk), 0)
    coli = lax.broadcasted_iota(jnp.int32, (tq, tk), 1)
    outs = []
    for j in range(2):
        hs = slice(j * dh, (j + 1) * dh)
        q = q_ref[0, :, hs]
        cq = cq_ref[0, 0, :, j:j + 1]
        m_ref[...] = jnp.full_like(m_ref, NEG_BIG)
        l_ref[...] = jnp.zeros_like(l_ref)
        acc_ref[...] = jnp.zeros_like(acc_ref)

        def step(ki, masked):
            ks = pl.ds(pl.multiple_of(ki * tk, tk), tk)
            s = lax.dot_general(q, k_ref[0, ks, hs], nt, preferred_element_type=F32)
            s = s + cq - ck_ref[0, 0, j:j + 1, ks]
            if masked:
                s = jnp.where(coli <= rowi, s, NEG_BIG)
            m_old = m_ref[...]
            m_new = jnp.maximum(m_old, jnp.max(s, axis=-1, keepdims=True))
            alpha = jnp.exp(m_old - m_new)
            p = jnp.exp(s - m_new)
            l_ref[...] = alpha * l_ref[...] + jnp.sum(p, axis=-1, keepdims=True)
            acc_ref[...] = alpha * acc_ref[...] + jnp.dot(p.astype(BF16), v_ref[0, ks, hs],
                                                          preferred_element_type=F32)
            m_ref[...] = m_new

        def body(ki, c):
            step(ki, False)
            return c

        lax.fori_loop(0, qi, body, 0)
        step(qi, True)
        outs.append(acc_ref[...] / l_ref[...])
    o = jnp.concatenate(outs, axis=-1) * gate_ref[0].astype(F32)
    o_ref[0] = o.astype(o_ref.dtype)


def fox_attention(q, k, v, cq, ck, gate, *, tq=256):
    bsz, s, hd = q.shape
    hp = FOX_HEADS // 2
    w = 2 * FOX_HEAD_DIM
    qspec = pl.BlockSpec((1, tq, w), lambda b, h, t: (b, t, h))
    kvspec = pl.BlockSpec((1, s, w), lambda b, h, t: (b, 0, h))
    return pl.pallas_call(
        functools.partial(_fox_kernel, tq=tq, tk=tq),
        grid=(bsz, hp, s // tq),
        in_specs=[qspec, kvspec, kvspec,
                  pl.BlockSpec((1, 1, tq, 2), lambda b, h, t: (b, h, t, 0)),
                  pl.BlockSpec((1, 1, 2, s), lambda b, h, t: (b, h, 0, 0)),
                  qspec],
        out_specs=qspec,
        out_shape=jax.ShapeDtypeStruct((bsz, s, hd), BF16),
        scratch_shapes=[pltpu.VMEM((tq, 1), F32), pltpu.VMEM((tq, 1), F32), pltpu.VMEM((tq, FOX_HEAD_DIM), F32)],
        compiler_params=_params("parallel", "parallel", "arbitrary"),
        name="fox_attention",
    )(q, k, v, cq, ck, gate)


def kernel(x, hg_w_in, hg_lb_logits, hg_norm_w, hg_w_out, kv_w, kv_fgate_b, fox_w_qg, fox_w_out, ln_mix_g, ln_mix_b, ln_ffn_g, ln_ffn_b, ffn_w_gu, ffn_w_down, moe_w_router, moe_w_gu, moe_w_down):
    bsz, s, d = x.shape
    depth = ln_mix_g.shape[0]
    n_a = hg_w_in.shape[0]
    alpha = (2 * depth) ** 0.25
    fox_dim = FOX_HEADS * FOX_HEAD_DIM
    t = bsz * s

    p = jax.nn.softmax(hg_lb_logits.astype(F32), axis=0)
    lbs = jnp.cumsum(p, axis=0) - p[0]

    def vec(a):
        return a.reshape(1, -1).astype(F32)

    x2 = x.reshape(t, d)
    k_sh = v_sh = cq = ck = None
    for layer in range(depth):
        if layer < n_a:
            q, kk, v, lf, g = hgrn_proj(x2, hg_w_in[layer].astype(BF16), vec(lbs[layer]))
            sh = (bsz, s, -1)
            o = hgrn_recurrence(q.reshape(sh), kk.reshape(sh), v.reshape(sh), lf.reshape(sh), g.reshape(sh),
                                vec(hg_norm_w[layer]))
            x2 = mm_ln(o.reshape(t, -1), hg_w_out[layer].astype(BF16), x2, vec(ln_mix_g[layer]),
                       vec(ln_mix_b[layer]), alpha=alpha)
        else:
            bi = layer - n_a
            w_qg = fox_w_qg[bi].astype(BF16)
            q = mm(x2, w_qg[:, :fox_dim], kind="scale", scale=FOX_HEAD_DIM ** -0.5)
            gate = mm(x2, w_qg[:, fox_dim:], kind="sigmoid")
            sh = (bsz, s, fox_dim)
            o = fox_attention(q.reshape(sh), k_sh, v_sh, cq, ck, gate.reshape(sh))
            x2 = mm_ln(o.reshape(t, fox_dim), fox_w_out[bi].astype(BF16), x2, vec(ln_mix_g[layer]),
                       vec(ln_mix_b[layer]), alpha=alpha)
        if layer % 2 == 0:
            x2 = ffn_ln(x2, ffn_w_gu[layer // 2].astype(BF16), ffn_w_down[layer // 2].astype(BF16),
                        vec(ln_ffn_g[layer]), vec(ln_ffn_b[layer]), alpha=alpha)
        else:
            w_r = jnp.pad(moe_w_router[layer // 2].astype(F32), ((0, 0), (0, LANES - N_EXPERTS)))
            w_r_hi = w_r.astype(BF16)
            w_r_lo = (w_r - w_r_hi.astype(F32)).astype(BF16)
            x2 = moe_ln(x2, w_r_hi, w_r_lo, moe_w_gu[layer // 2].astype(BF16), moe_w_down[layer // 2].astype(BF16),
                        vec(ln_ffn_g[layer]), vec(ln_ffn_b[layer]), alpha=alpha)
        if layer == n_a - 1:
            w_kv = kv_w.astype(BF16)
            k_sh = mm(x2, w_kv[:, :fox_dim]).reshape(bsz, s, fox_dim)
            v_sh = mm(x2, w_kv[:, fox_dim:2 * fox_dim]).reshape(bsz, s, fox_dim)
            w_f = jnp.pad(w_kv[:, 2 * fox_dim:], ((0, 0), (0, LANES - FOX_HEADS)))
            b_f = jnp.pad(kv_fgate_b.astype(F32), (0, LANES - FOX_HEADS)).reshape(1, LANES)
            log_f = mm(x2, w_f, kind="log_sigmoid_bias", bias=b_f, out_dtype=F32)
            c = seq_cumsum(log_f.reshape(bsz, s, LANES))[:, :, :FOX_HEADS]
            c = c.reshape(bsz, s, FOX_HEADS // 2, 2)
            cq = c.transpose(0, 2, 1, 3)
            ck = c.transpose(0, 2, 3, 1)
    return x2.reshape(bsz, s, d)
```

```python
import functools

import jax
import jax.numpy as jnp
from jax import lax
from jax.experimental import pallas as pl
from jax.experimental.pallas import tpu as pltpu

F32 = jnp.float32
BF16 = jnp.bfloat16

HG_HEADS = 8
HG_CHUNK = 64
HG_SUB = 16
LB_FLOOR = 1e-30
FOX_HEADS = 16
FOX_HEAD_DIM = 64
N_EXPERTS = 8
TOP_K = 2
LN_EPS = 1e-5
RMS_EPS = 1e-6

LANES = 128
VMEM_LIMIT_BYTES = 56 * 1024 * 1024

NEG_BIG = -1e30


def _params(*sem):
    return pltpu.CompilerParams(dimension_semantics=sem, vmem_limit_bytes=VMEM_LIMIT_BYTES)


def _layer_norm(h, g, b):
    mu = jnp.mean(h, axis=-1, keepdims=True)
    d = h - mu
    var = jnp.mean(d * d, axis=-1, keepdims=True)
    return d * lax.rsqrt(var + LN_EPS) * g + b


def _silu(x):
    return x * jax.nn.sigmoid(x)


def _log1p_exp_neg(d):
    return jnp.log(1.0 + jnp.exp(-d))


def _log_sigmoid(z):
    return jnp.minimum(z, 0.0) - _log1p_exp_neg(jnp.abs(z))


def _split3(x):
    hi = x.astype(BF16)
    r1 = x - hi.astype(F32)
    mid = r1.astype(BF16)
    lo = (r1 - mid.astype(F32)).astype(BF16)
    return hi, mid, lo


def _tri(n):
    r = lax.broadcasted_iota(jnp.int32, (n, n), 0)
    c = lax.broadcasted_iota(jnp.int32, (n, n), 1)
    return (r >= c).astype(BF16)


def _cumsum_rows(x, tri):
    hi, mid, lo = _split3(x)
    return (jnp.dot(tri, hi, preferred_element_type=F32)
            + jnp.dot(tri, mid, preferred_element_type=F32)
            + jnp.dot(tri, lo, preferred_element_type=F32))


def _hgrn_proj_kernel(x_ref, w_ref, lb_ref, q_ref, k_ref, v_ref, lf_ref, g_ref):
    j = pl.program_id(1)
    acc = jnp.dot(x_ref[...].astype(BF16), w_ref[...], preferred_element_type=F32)

    @pl.when(j == 0)
    def _():
        q_ref[...] = _silu(acc)

    @pl.when(j == 1)
    def _():
        lb = lb_ref[...]
        log_lb = jnp.log(jnp.maximum(lb, LB_FLOOR))
        a = _log_sigmoid(acc)
        b = log_lb + _log_sigmoid(-acc)
        lf_ref[...] = jnp.maximum(a, b) + _log1p_exp_neg(jnp.abs(a - b))
        k_ref[...] = (1.0 - lb) * jax.nn.sigmoid(-acc)

    @pl.when(j == 2)
    def _():
        v_ref[...] = acc

    @pl.when(j == 3)
    def _():
        g_ref[...] = _silu(acc).astype(g_ref.dtype)


def hgrn_proj(x2, w_in, lb, *, tm=512):
    m, d = x2.shape
    n = w_in.shape[1] // 4
    row = lambda i, j: (i, 0)
    out_sd = jax.ShapeDtypeStruct((m, n), F32)
    return pl.pallas_call(
        _hgrn_proj_kernel,
        grid=(m // tm, 4),
        in_specs=[pl.BlockSpec((tm, d), row),
                  pl.BlockSpec((d, n), lambda i, j: (0, j)),
                  pl.BlockSpec((1, n), lambda i, j: (0, 0))],
        out_specs=[pl.BlockSpec((tm, n), row)] * 5,
        out_shape=[out_sd, out_sd, out_sd, out_sd, jax.ShapeDtypeStruct((m, n), BF16)],
        compiler_params=_params("parallel", "arbitrary"),
        name="hgrn_proj",
    )(x2, w_in, lb)


def _hgrn_chunk(qs, ks, vs, lfs, sts, tri):
    heads = range(len(qs))
    c = qs[0].shape[0]
    nblk = c // HG_SUB
    nt = (((1,), (1,)), ((), ()))
    tn = (((0,), (0,)), ((), ()))
    bs = [_cumsum_rows(lfs[h], tri) for h in heads]
    o_in = [lax.dot_general((qs[h] * jnp.exp(bs[h])).astype(BF16), sts[h].astype(BF16), nt,
                            preferred_element_type=F32) for h in heads]
    b_last = [bs[h][c - 1:c] for h in heads]
    upd = [lax.dot_general(vs[h].astype(BF16), (ks[h] * jnp.exp(b_last[h] - bs[h])).astype(BF16), tn,
                           preferred_element_type=F32) for h in heads]
    st_new = [sts[h] * jnp.exp(b_last[h]) + upd[h] for h in heads]

    p = {}
    for i in range(1, nblk):
        lo, hi = i * HG_SUB, (i + 1) * HG_SUB
        for h in heads:
            r = bs[h][lo - 1:lo]
            qd = qs[h][lo:hi] * jnp.exp(bs[h][lo:hi] - r)
            kp = ks[h][:lo] * jnp.exp(r - bs[h][:lo])
            p[i, h] = lax.dot_general(qd.astype(BF16), kp.astype(BF16), nt, preferred_element_type=F32)
    off = {}
    for i in range(1, nblk):
        lo = i * HG_SUB
        for h in heads:
            off[i, h] = jnp.dot(p[i, h].astype(BF16), vs[h][:lo].astype(BF16), preferred_element_type=F32)

    row = lax.broadcasted_iota(jnp.int32, (HG_SUB, 1), 0)
    outs = []
    for h in heads:
        blocks = []
        for i in range(nblk):
            lo, hi = i * HG_SUB, (i + 1) * HG_SUB
            bi, qi, ki, vi = bs[h][lo:hi], qs[h][lo:hi], ks[h][lo:hi], vs[h][lo:hi]
            oi = o_in[h][lo:hi]
            if i > 0:
                oi = oi + off[i, h]
            for s in range(HG_SUB):
                e = jnp.exp(jnp.minimum(bi - bi[s:s + 1], 0.0))
                a = jnp.sum(qi * e * ki[s:s + 1], axis=-1, keepdims=True)
                a = jnp.where(row >= s, a, 0.0)
                oi = oi + a * vi[s:s + 1]
            blocks.append(oi)
        outs.append(jnp.concatenate(blocks, axis=0))
    return outs, st_new


def _hgrn_rec_kernel(q_ref, k_ref, v_ref, lf_ref, g_ref, nw_ref, o_ref, st_ref, *, n_chunks, dh):
    nh = st_ref.shape[0]

    @pl.when(pl.program_id(2) == 0)
    def _():
        st_ref[...] = jnp.zeros_like(st_ref)

    tri = _tri(HG_CHUNK)
    nw = nw_ref[...]

    def body(ci, carry):
        sl = pl.ds(pl.multiple_of(ci * HG_CHUNK, HG_CHUNK), HG_CHUNK)
        cols = [slice(h * dh, (h + 1) * dh) for h in range(nh)]
        outs, st_new = _hgrn_chunk([q_ref[0, sl, cs] for cs in cols], [k_ref[0, sl, cs] for cs in cols],
                                   [v_ref[0, sl, cs] for cs in cols], [lf_ref[0, sl, cs] for cs in cols],
                                   [st_ref[h] for h in range(nh)], tri)
        for h in range(nh):
            st_ref[h] = st_new[h]
            o = outs[h]
            o = o * lax.rsqrt(jnp.mean(o * o, axis=-1, keepdims=True) + RMS_EPS)
            o = o * nw * g_ref[0, sl, cols[h]].astype(F32)
            o_ref[0, sl, cols[h]] = o.astype(o_ref.dtype)
        return carry

    lax.fori_loop(0, n_chunks, body, 0)


def hgrn_recurrence(q, k, v, lf, g, norm_w, *, sblk=512, nhs=4):
    bsz, s, hd = q.shape
    dh = hd // HG_HEADS
    blk = pl.BlockSpec((1, sblk, nhs * dh), lambda b, h, t: (b, t, h))
    return pl.pallas_call(
        functools.partial(_hgrn_rec_kernel, n_chunks=sblk // HG_CHUNK, dh=dh),
        grid=(bsz, HG_HEADS // nhs, s // sblk),
        in_specs=[blk, blk, blk, blk, blk, pl.BlockSpec((1, dh), lambda b, h, t: (0, 0))],
        out_specs=blk,
        out_shape=jax.ShapeDtypeStruct((bsz, s, hd), BF16),
        scratch_shapes=[pltpu.VMEM((nhs, dh, dh), F32)],
        compiler_params=_params("parallel", "parallel", "arbitrary"),
        name="hgrn_recurrence",
    )(q, k, v, lf, g, norm_w)


def _mm_ln_kernel(h_ref, w_ref, x_ref, g_ref, b_ref, o_ref, *, alpha):
    y = jnp.dot(h_ref[...], w_ref[...], preferred_element_type=F32)
    o_ref[...] = _layer_norm(alpha * x_ref[...] + y, g_ref[...], b_ref[...])


def mm_ln(h, w, x2, g, b, *, alpha, tm=512):
    m, kdim = h.shape
    d = w.shape[1]
    row = lambda i: (i, 0)
    fixed = lambda i: (0, 0)
    return pl.pallas_call(
        functools.partial(_mm_ln_kernel, alpha=alpha),
        grid=(m // tm,),
        in_specs=[pl.BlockSpec((tm, kdim), row), pl.BlockSpec((kdim, d), fixed),
                  pl.BlockSpec((tm, d), row), pl.BlockSpec((1, d), fixed), pl.BlockSpec((1, d), fixed)],
        out_specs=pl.BlockSpec((tm, d), row),
        out_shape=jax.ShapeDtypeStruct((m, d), F32),
        compiler_params=_params("parallel"),
        name="mm_ln",
    )(h, w, x2, g, b)


def _ffn_ln_kernel(x_ref, wg_ref, wu_ref, wd_ref, g_ref, b_ref, o_ref, *, alpha):
    xb = x_ref[...].astype(BF16)
    a = jnp.dot(xb, wg_ref[...], preferred_element_type=F32)
    u = jnp.dot(xb, wu_ref[...], preferred_element_type=F32)
    y = jnp.dot((_silu(a) * u).astype(BF16), wd_ref[...], preferred_element_type=F32)
    o_ref[...] = _layer_norm(alpha * x_ref[...] + y, g_ref[...], b_ref[...])


def ffn_ln(x2, w_gu, w_down, g, b, *, alpha, tm=512):
    m, d = x2.shape
    f = w_down.shape[0]
    row = lambda i: (i, 0)
    fixed = lambda i: (0, 0)
    once = pl.Buffered(1)
    return pl.pallas_call(
        functools.partial(_ffn_ln_kernel, alpha=alpha),
        grid=(m // tm,),
        in_specs=[pl.BlockSpec((tm, d), row),
                  pl.BlockSpec((d, f), lambda i: (0, 0), pipeline_mode=once),
                  pl.BlockSpec((d, f), lambda i: (0, 1), pipeline_mode=once),
                  pl.BlockSpec((f, d), fixed, pipeline_mode=once),
                  pl.BlockSpec((1, d), fixed), pl.BlockSpec((1, d), fixed)],
        out_specs=pl.BlockSpec((tm, d), row),
        out_shape=jax.ShapeDtypeStruct((m, d), F32),
        compiler_params=_params("parallel"),
        name="ffn_ln",
    )(x2, w_gu, w_gu, w_down, g, b)


def _router_kernel(x_ref, wh_ref, wl_ref, o_ref):
    x = x_ref[...]
    xh = x.astype(BF16)
    xl = (x - xh.astype(F32)).astype(BF16)
    wh, wl = wh_ref[...], wl_ref[...]
    logits = (jnp.dot(xh, wh, preferred_element_type=F32)
              + (jnp.dot(xl, wh, preferred_element_type=F32) + jnp.dot(xh, wl, preferred_element_type=F32)))
    lane = lax.broadcasted_iota(jnp.int32, logits.shape, 1)
    logits = jnp.where(lane < N_EXPERTS, logits, -jnp.inf)
    m1 = jnp.max(logits, axis=-1, keepdims=True)
    i1 = jnp.min(jnp.where(logits == m1, lane, LANES), axis=-1, keepdims=True)
    rest = jnp.where(lane == i1, -jnp.inf, logits)
    m2 = jnp.max(rest, axis=-1, keepdims=True)
    i2 = jnp.min(jnp.where(rest == m2, lane, LANES), axis=-1, keepdims=True)
    e2 = jnp.exp(m2 - m1)
    den = 1.0 + e2
    w1 = 1.0 / den
    w2 = e2 / den
    o_ref[...] = jnp.where(lane == 0, i1.astype(F32),
                           jnp.where(lane == 1, i2.astype(F32),
                                     jnp.where(lane == 2, w1, jnp.where(lane == 3, w2, 0.0))))


def router(x2, w_hi, w_lo, *, tm=1024):
    m, d = x2.shape
    row = lambda i: (i, 0)
    fixed = lambda i: (0, 0)
    return pl.pallas_call(
        _router_kernel,
        grid=(m // tm,),
        in_specs=[pl.BlockSpec((tm, d), row), pl.BlockSpec((d, LANES), fixed), pl.BlockSpec((d, LANES), fixed)],
        out_specs=pl.BlockSpec((tm, LANES), row),
        out_shape=jax.ShapeDtypeStruct((m, LANES), F32),
        compiler_params=_params("parallel"),
        name="moe_router",
    )(x2, w_hi, w_lo)


def _row_copy(src_hbm, dst_vmem, sem, src_row, dst_row):
    return pltpu.make_async_copy(src_hbm.at[pl.ds(src_row, 1)], dst_vmem.at[pl.ds(dst_row, 1)], sem)


def _gather_start(idx_ref, base, src_hbm, dst_vmem, sem, n_rows):
    for r in range(n_rows):
        _row_copy(src_hbm, dst_vmem, sem, idx_ref[base + r], r).start()


def _gather_wait(src_hbm, dst_vmem, sem, n_rows):
    pltpu.make_async_copy(src_hbm.at[pl.ds(0, n_rows)], dst_vmem, sem).wait()


def _gather_kernel(idx_ref, src_hbm, o_ref, sem, *, rows):
    _gather_start(idx_ref, pl.program_id(0) * rows, src_hbm, o_ref, sem, rows)
    _gather_wait(src_hbm, o_ref, sem, rows)


def gather_rows(src, idx, *, rows=1024):
    n = idx.shape[0]
    d = src.shape[1]
    return pl.pallas_call(
        functools.partial(_gather_kernel, rows=rows),
        grid_spec=pltpu.PrefetchScalarGridSpec(
            num_scalar_prefetch=1,
            grid=(n // rows,),
            in_specs=[pl.BlockSpec(memory_space=pl.ANY)],
            out_specs=pl.BlockSpec((rows, d), lambda i, idx: (i, 0)),
            scratch_shapes=[pltpu.SemaphoreType.DMA(())]),
        out_shape=jax.ShapeDtypeStruct((n, d), src.dtype),
        compiler_params=_params("arbitrary"),
        name="moe_gather",
    )(idx, src)


def _experts_kernel(te_ref, nu_ref, xs_ref, wg_ref, wu_ref, wd_ref, o_ref, acc_ref):
    r, c = pl.program_id(0), pl.program_id(1)

    @pl.when(r < nu_ref[0])
    def _():
        xb = xs_ref[...].astype(BF16)
        a = jnp.dot(xb, wg_ref[0], preferred_element_type=F32)
        u = jnp.dot(xb, wu_ref[0], preferred_element_type=F32)
        part = jnp.dot((_silu(a) * u).astype(BF16), wd_ref[0], preferred_element_type=F32)

        @pl.when(c == 0)
        def _():
            acc_ref[...] = part

        @pl.when(c > 0)
        def _():
            acc_ref[...] += part

        @pl.when(c == pl.num_programs(1) - 1)
        def _():
            o_ref[...] = acc_ref[...]

    @pl.when(jnp.logical_and(r >= nu_ref[0], c == pl.num_programs(1) - 1))
    def _():
        o_ref[...] = jnp.zeros_like(o_ref)


def experts(xs, w_gu, w_down, tile_expert, n_used, *, tm, th=1792):
    n, d = xs.shape
    f = w_down.shape[1]
    nc = f // th
    return pl.pallas_call(
        _experts_kernel,
        grid_spec=pltpu.PrefetchScalarGridSpec(
            num_scalar_prefetch=2,
            grid=(n // tm, nc),
            in_specs=[pl.BlockSpec((tm, d), lambda r, c, te, nu: (r, 0)),
                      pl.BlockSpec((1, d, th), lambda r, c, te, nu: (te[r], 0, c)),
                      pl.BlockSpec((1, d, th), lambda r, c, te, nu: (te[r], 0, nc + c)),
                      pl.BlockSpec((1, th, d), lambda r, c, te, nu: (te[r], c, 0))],
            out_specs=pl.BlockSpec((tm, d), lambda r, c, te, nu: (r, 0)),
            scratch_shapes=[pltpu.VMEM((tm, d), F32)]),
        out_shape=jax.ShapeDtypeStruct((n, d), F32),
        compiler_params=_params("arbitrary", "arbitrary"),
        name="moe_experts",
    )(tile_expert, n_used, xs, w_gu, w_gu, w_down)


def _combine_ln_kernel(s1_ref, s2_ref, ys_hbm, x_ref, rw_ref, g_ref, b_ref, o_ref, a_buf, b_buf, sem, *,
                       alpha, rows):
    base = pl.program_id(0) * rows
    _gather_start(s1_ref, base, ys_hbm, a_buf, sem.at[0], rows)
    _gather_start(s2_ref, base, ys_hbm, b_buf, sem.at[1], rows)
    _gather_wait(ys_hbm, a_buf, sem.at[0], rows)
    _gather_wait(ys_hbm, b_buf, sem.at[1], rows)
    rw = rw_ref[...]
    y = rw[:, 2:3] * a_buf[...] + rw[:, 3:4] * b_buf[...]
    o_ref[...] = _layer_norm(alpha * x_ref[...] + y, g_ref[...], b_ref[...])


def combine_ln(ys, slot1, slot2, x2, rw, g, b, *, alpha, rows=512):
    m, d = x2.shape
    row = lambda i, s1, s2: (i, 0)
    fixed = lambda i, s1, s2: (0, 0)
    return pl.pallas_call(
        functools.partial(_combine_ln_kernel, alpha=alpha, rows=rows),
        grid_spec=pltpu.PrefetchScalarGridSpec(
            num_scalar_prefetch=2,
            grid=(m // rows,),
            in_specs=[pl.BlockSpec(memory_space=pl.ANY),
                      pl.BlockSpec((rows, d), row), pl.BlockSpec((rows, LANES), row),
                      pl.BlockSpec((1, d), fixed), pl.BlockSpec((1, d), fixed)],
            out_specs=pl.BlockSpec((rows, d), row),
            scratch_shapes=[pltpu.VMEM((rows, d), F32), pltpu.VMEM((rows, d), F32),
                            pltpu.SemaphoreType.DMA((2,))]),
        out_shape=jax.ShapeDtypeStruct((m, d), F32),
        compiler_params=_params("arbitrary"),
        name="moe_combine_ln",
    )(slot1, slot2, ys, x2, rw, g, b)


def moe_ln(x2, w_r_hi, w_r_lo, w_gu, w_down, g, b, *, alpha, tm=512):
    t = x2.shape[0]
    rw = router(x2, w_r_hi, w_r_lo)
    e_idx = rw[:, :TOP_K].astype(jnp.int32)
    flat_e = e_idx.reshape(-1)
    onehot = (flat_e[:, None] == jnp.arange(N_EXPERTS, dtype=jnp.int32)[None, :]).astype(jnp.int32)
    rank = jnp.take_along_axis(jnp.cumsum(onehot, axis=0) - onehot, flat_e[:, None], axis=1)[:, 0]
    counts = jnp.sum(onehot, axis=0)
    padded = ((counts + tm - 1) // tm) * tm
    ends = jnp.cumsum(padded)
    starts = ends - padded
    slot = starts[flat_e] + rank
    n_tiles = (TOP_K * t) // tm + N_EXPERTS
    n_slots = n_tiles * tm
    token = jnp.arange(TOP_K * t, dtype=jnp.int32) // TOP_K
    token_of_slot = jnp.zeros((n_slots,), jnp.int32).at[slot].set(token)
    tile_start = jnp.arange(n_tiles, dtype=jnp.int32) * tm
    tile_expert = jnp.minimum(jnp.sum((tile_start[:, None] >= ends[None, :]).astype(jnp.int32), axis=1),
                              N_EXPERTS - 1).astype(jnp.int32)
    n_used = (ends[-1:] // tm).astype(jnp.int32)

    xs = gather_rows(x2, token_of_slot)
    ys = experts(xs, w_gu, w_down, tile_expert, n_used, tm=tm)
    slot2d = slot.reshape(t, TOP_K).astype(jnp.int32)
    return combine_ln(ys, slot2d[:, 0], slot2d[:, 1], x2, rw, g, b, alpha=alpha)


def _mm_kernel(x_ref, w_ref, bias_ref, o_ref, *, kind, scale):
    acc = jnp.dot(x_ref[...].astype(BF16), w_ref[...], preferred_element_type=F32)
    if kind == "scale":
        acc = acc * scale
    elif kind == "sigmoid":
        acc = jax.nn.sigmoid(acc)
    elif kind == "log_sigmoid_bias":
        acc = _log_sigmoid(acc + bias_ref[...])
    o_ref[...] = acc.astype(o_ref.dtype)


def mm(x2, w, *, kind="none", scale=1.0, bias=None, out_dtype=BF16, tm=1024, tn=None):
    m, d = x2.shape
    n = w.shape[1]
    tn = n if tn is None else tn
    if bias is None:
        bias = jnp.zeros((1, n), F32)
    return pl.pallas_call(
        functools.partial(_mm_kernel, kind=kind, scale=scale),
        grid=(m // tm, n // tn),
        in_specs=[pl.BlockSpec((tm, d), lambda i, j: (i, 0)), pl.BlockSpec((d, tn), lambda i, j: (0, j)),
                  pl.BlockSpec((1, tn), lambda i, j: (0, j))],
        out_specs=pl.BlockSpec((tm, tn), lambda i, j: (i, j)),
        out_shape=jax.ShapeDtypeStruct((m, n), out_dtype),
        compiler_params=_params("parallel", "arbitrary"),
        name="mm_" + kind,
    )(x2, w, bias)


def _seq_cumsum_kernel(x_ref, o_ref, *, blk):
    tri = _tri(blk)
    n = x_ref.shape[1] // blk

    def body(i, carry):
        sl = pl.ds(pl.multiple_of(i * blk, blk), blk)
        c = _cumsum_rows(x_ref[0, sl, :], tri) + carry
        o_ref[0, sl, :] = c
        return c[blk - 1:blk]

    lax.fori_loop(0, n, body, jnp.zeros((1, x_ref.shape[2]), F32))


def seq_cumsum(x, *, blk=256):
    bsz, s, n = x.shape
    spec = pl.BlockSpec((1, s, n), lambda b: (b, 0, 0))
    return pl.pallas_call(
        functools.partial(_seq_cumsum_kernel, blk=blk),
        grid=(bsz,), in_specs=[spec], out_specs=spec,
        out_shape=jax.ShapeDtypeStruct(x.shape, F32),
        compiler_params=_params("parallel"),
        name="seq_cumsum",
    )(x)


def _fox_kernel(qt_ref, ka_ref, vt_ref, gate_ref, o_ref, acc_ref, *, tq, tk):
    qi = pl.program_id(2)
    rowi = lax.broadcasted_iota(jnp.int32, (tk, tq), 0)
    coli = lax.broadcasted_iota(jnp.int32, (tk, tq), 1)
    acc_ref[...] = jnp.zeros_like(acc_ref)

    def tile(ki, stats, masked):
        ks = pl.ds(pl.multiple_of(ki * tk, tk), tk)
        heads = range(len(stats))
        ss = [jnp.dot(ka_ref[0, j, ks, :], qt_ref[0, j], preferred_element_type=F32) for j in heads]
        if masked:
            ss = [jnp.where(rowi <= coli, s, NEG_BIG) for s in ss]
        new, alphas, ps = [], [], []
        for j in heads:
            m_old, l_old = stats[j]
            m_new = jnp.maximum(m_old, jnp.max(ss[j], axis=0, keepdims=True))
            alpha = jnp.exp(m_old - m_new)
            p = jnp.exp(ss[j] - m_new)
            new.append((m_new, alpha * l_old + jnp.sum(p, axis=0, keepdims=True)))
            alphas.append(alpha)
            ps.append(p.astype(BF16))
        pvs = [jnp.dot(vt_ref[0, j, :, ks], ps[j], preferred_element_type=F32) for j in heads]
        for j in heads:
            acc_ref[j] = alphas[j] * acc_ref[j] + pvs[j]
        return tuple(new)

    nhs = acc_ref.shape[0]
    init = tuple((jnp.full((1, tq), NEG_BIG, F32), jnp.zeros((1, tq), F32)) for _ in range(nhs))
    stats = lax.fori_loop(0, qi, lambda ki, st: tile(ki, st, False), init)
    stats = tile(qi, stats, True)
    ot = jnp.concatenate([acc_ref[j] / stats[j][1] for j in range(nhs)], axis=0)
    o_ref[0] = (ot.T * gate_ref[0].astype(F32)).astype(o_ref.dtype)


def fox_attention(qt, ka, vt, gate, *, tq=256, nhs=8):
    bsz, nh, aug, s = qt.shape
    dh = vt.shape[2]
    hp = nh // nhs
    return pl.pallas_call(
        functools.partial(_fox_kernel, tq=tq, tk=tq),
        grid=(bsz, hp, s // tq),
        in_specs=[pl.BlockSpec((1, nhs, aug, tq), lambda b, h, t: (b, h, 0, t)),
                  pl.BlockSpec((1, nhs, s, aug), lambda b, h, t: (b, h, 0, 0)),
                  pl.BlockSpec((1, nhs, dh, s), lambda b, h, t: (b, h, 0, 0)),
                  pl.BlockSpec((1, tq, nhs * dh), lambda b, h, t: (b, t, h))],
        out_specs=pl.BlockSpec((1, tq, nhs * dh), lambda b, h, t: (b, t, h)),
        out_shape=jax.ShapeDtypeStruct((bsz, s, nh * dh), BF16),
        scratch_shapes=[pltpu.VMEM((nhs, dh, tq), F32)],
        compiler_params=_params("parallel", "parallel", "arbitrary"),
        name="fox_attention",
    )(qt, ka, vt, gate)


def _split3_cols(c):
    return jnp.concatenate(_split3(c), axis=-1)


def fox_prep_kv(k_sh, v_sh, c):
    bsz, s, _ = k_sh.shape
    nh, dh = FOX_HEADS, FOX_HEAD_DIM
    k4 = k_sh.reshape(bsz, s, nh, dh).transpose(0, 2, 1, 3)
    c3 = _split3_cols(c.transpose(0, 2, 1)[..., None])
    ones = jnp.ones((bsz, nh, s, 3), BF16)
    pad = jnp.zeros((bsz, nh, s, LANES - dh - 6), BF16)
    ka = jnp.concatenate([k4, c3, ones, pad], axis=-1)
    vt = v_sh.reshape(bsz, s, nh, dh).transpose(0, 2, 3, 1)
    cq_rows = jnp.concatenate([-ones, c3, pad], axis=-1).transpose(0, 1, 3, 2)
    return ka, vt, cq_rows


def fox_prep_q(q, cq_rows):
    bsz, s, _ = q.shape
    qt = q.reshape(bsz, s, FOX_HEADS, FOX_HEAD_DIM).transpose(0, 2, 3, 1)
    return jnp.concatenate([qt, cq_rows], axis=2)


def kernel(x, hg_w_in, hg_lb_logits, hg_norm_w, hg_w_out, kv_w, kv_fgate_b, fox_w_qg, fox_w_out, ln_mix_g, ln_mix_b, ln_ffn_g, ln_ffn_b, ffn_w_gu, ffn_w_down, moe_w_router, moe_w_gu, moe_w_down):
    bsz, s, d = x.shape
    depth = ln_mix_g.shape[0]
    n_a = hg_w_in.shape[0]
    alpha = (2 * depth) ** 0.25
    fox_dim = FOX_HEADS * FOX_HEAD_DIM
    t = bsz * s

    p = jax.nn.softmax(hg_lb_logits.astype(F32), axis=0)
    lbs = jnp.cumsum(p, axis=0) - p[0]

    def vec(a):
        return a.reshape(1, -1).astype(F32)

    x2 = x.reshape(t, d)
    ka = vt = cq_rows = None
    for layer in range(depth):
        if layer < n_a:
            q, kk, v, lf, g = hgrn_proj(x2, hg_w_in[layer].astype(BF16), vec(lbs[layer]))
            sh = (bsz, s, -1)
            o = hgrn_recurrence(q.reshape(sh), kk.reshape(sh), v.reshape(sh), lf.reshape(sh), g.reshape(sh),
                                vec(hg_norm_w[layer]))
            x2 = mm_ln(o.reshape(t, -1), hg_w_out[layer].astype(BF16), x2, vec(ln_mix_g[layer]),
                       vec(ln_mix_b[layer]), alpha=alpha)
        else:
            bi = layer - n_a
            w_qg = fox_w_qg[bi].astype(BF16)
            q = mm(x2, w_qg[:, :fox_dim], kind="scale", scale=FOX_HEAD_DIM ** -0.5)
            gate = mm(x2, w_qg[:, fox_dim:], kind="sigmoid")
            sh = (bsz, s, fox_dim)
            o = fox_attention(fox_prep_q(q.reshape(sh), cq_rows), ka, vt, gate.reshape(sh))
            x2 = mm_ln(o.reshape(t, fox_dim), fox_w_out[bi].astype(BF16), x2, vec(ln_mix_g[layer]),
                       vec(ln_mix_b[layer]), alpha=alpha)
        if layer % 2 == 0:
            x2 = ffn_ln(x2, ffn_w_gu[layer // 2].astype(BF16), ffn_w_down[layer // 2].astype(BF16),
                        vec(ln_ffn_g[layer]), vec(ln_ffn_b[layer]), alpha=alpha)
        else:
            w_r = jnp.pad(moe_w_router[layer // 2].astype(F32), ((0, 0), (0, LANES - N_EXPERTS)))
            w_r_hi = w_r.astype(BF16)
            w_r_lo = (w_r - w_r_hi.astype(F32)).astype(BF16)
            x2 = moe_ln(x2, w_r_hi, w_r_lo, moe_w_gu[layer // 2].astype(BF16), moe_w_down[layer // 2].astype(BF16),
                        vec(ln_ffn_g[layer]), vec(ln_ffn_b[layer]), alpha=alpha)
        if layer == n_a - 1:
            w_kv = kv_w.astype(BF16)
            k_sh = mm(x2, w_kv[:, :fox_dim]).reshape(bsz, s, fox_dim)
            v_sh = mm(x2, w_kv[:, fox_dim:2 * fox_dim]).reshape(bsz, s, fox_dim)
            w_f = jnp.pad(w_kv[:, 2 * fox_dim:], ((0, 0), (0, LANES - FOX_HEADS)))
            b_f = jnp.pad(kv_fgate_b.astype(F32), (0, LANES - FOX_HEADS)).reshape(1, LANES)
            log_f = mm(x2, w_f, kind="log_sigmoid_bias", bias=b_f, out_dtype=F32)
            c = seq_cumsum(log_f.reshape(bsz, s, LANES))[:, :, :FOX_HEADS]
            ka, vt, cq_rows = fox_prep_kv(k_sh, v_sh, c)
    return x2.reshape(bsz, s, d)
```

```python
import functools

import jax
import jax.numpy as jnp
from jax import lax
from jax.experimental import pallas as pl
from jax.experimental.pallas import tpu as pltpu

F32 = jnp.float32
BF16 = jnp.bfloat16

HG_HEADS = 8
HG_CHUNK = 64
HG_SUB = 16
LB_FLOOR = 1e-30
FOX_HEADS = 16
FOX_HEAD_DIM = 64
FOX_HEADS_PER_STEP = 8
N_EXPERTS = 8
TOP_K = 2
LN_EPS = 1e-5
RMS_EPS = 1e-6

LANES = 128
VMEM_LIMIT_BYTES = 56 * 1024 * 1024

NEG_BIG = -1e30


def _params(*sem):
    return pltpu.CompilerParams(dimension_semantics=sem, vmem_limit_bytes=VMEM_LIMIT_BYTES)


def _layer_norm(h, g, b):
    mu = jnp.mean(h, axis=-1, keepdims=True)
    d = h - mu
    var = jnp.mean(d * d, axis=-1, keepdims=True)
    return d * lax.rsqrt(var + LN_EPS) * g + b


def _silu(x):
    return x * jax.nn.sigmoid(x)


def _log1p_exp_neg(d):
    return jnp.log(1.0 + jnp.exp(-d))


def _log_sigmoid(z):
    return jnp.minimum(z, 0.0) - _log1p_exp_neg(jnp.abs(z))


def _split3(x):
    hi = x.astype(BF16)
    r1 = x - hi.astype(F32)
    mid = r1.astype(BF16)
    lo = (r1 - mid.astype(F32)).astype(BF16)
    return hi, mid, lo


def _tri(n):
    r = lax.broadcasted_iota(jnp.int32, (n, n), 0)
    c = lax.broadcasted_iota(jnp.int32, (n, n), 1)
    return (r >= c).astype(BF16)


def _cumsum_rows(x, tri):
    hi, mid, lo = _split3(x)
    return (jnp.dot(tri, hi, preferred_element_type=F32)
            + jnp.dot(tri, mid, preferred_element_type=F32)
            + jnp.dot(tri, lo, preferred_element_type=F32))


def _hgrn_proj_kernel(x_ref, w_ref, lb_ref, q_ref, k_ref, v_ref, lf_ref, g_ref):
    j = pl.program_id(1)
    acc = jnp.dot(x_ref[...].astype(BF16), w_ref[...], preferred_element_type=F32)

    @pl.when(j == 0)
    def _():
        q_ref[...] = _silu(acc).astype(q_ref.dtype)

    @pl.when(j == 1)
    def _():
        lb = lb_ref[...]
        log_lb = jnp.log(jnp.maximum(lb, LB_FLOOR))
        a = _log_sigmoid(acc)
        b = log_lb + _log_sigmoid(-acc)
        lf_ref[...] = jnp.maximum(a, b) + _log1p_exp_neg(jnp.abs(a - b))
        k_ref[...] = ((1.0 - lb) * jax.nn.sigmoid(-acc)).astype(k_ref.dtype)

    @pl.when(j == 2)
    def _():
        v_ref[...] = acc.astype(v_ref.dtype)

    @pl.when(j == 3)
    def _():
        g_ref[...] = _silu(acc).astype(g_ref.dtype)


def hgrn_proj(x2, w_in, lb, *, tm=512):
    m, d = x2.shape
    n = w_in.shape[1] // 4
    row = lambda i, j: (i, 0)
    out_sd = jax.ShapeDtypeStruct((m, n), F32)
    out_bf = jax.ShapeDtypeStruct((m, n), BF16)
    return pl.pallas_call(
        _hgrn_proj_kernel,
        grid=(m // tm, 4),
        in_specs=[pl.BlockSpec((tm, d), row),
                  pl.BlockSpec((d, n), lambda i, j: (0, j)),
                  pl.BlockSpec((1, n), lambda i, j: (0, 0))],
        out_specs=[pl.BlockSpec((tm, n), row)] * 5,
        out_shape=[out_bf, out_bf, out_bf, out_sd, out_bf],
        compiler_params=_params("parallel", "arbitrary"),
        name="hgrn_proj",
    )(x2, w_in, lb)


def _hgrn_chunk(qs, ks, vs, lfs, sts, tri):
    heads = range(len(qs))
    c = qs[0].shape[0]
    nblk = c // HG_SUB
    nt = (((1,), (1,)), ((), ()))
    tn = (((0,), (0,)), ((), ()))
    bs = [_cumsum_rows(lfs[h], tri) for h in heads]
    o_in = [lax.dot_general((qs[h] * jnp.exp(bs[h])).astype(BF16), sts[h].astype(BF16), nt,
                            preferred_element_type=F32) for h in heads]
    b_last = [bs[h][c - 1:c] for h in heads]
    upd = [lax.dot_general(vs[h].astype(BF16), (ks[h] * jnp.exp(b_last[h] - bs[h])).astype(BF16), tn,
                           preferred_element_type=F32) for h in heads]
    st_new = [sts[h] * jnp.exp(b_last[h]) + upd[h] for h in heads]

    p = {}
    for i in range(1, nblk):
        lo, hi = i * HG_SUB, (i + 1) * HG_SUB
        for h in heads:
            r = bs[h][lo - 1:lo]
            qd = qs[h][lo:hi] * jnp.exp(bs[h][lo:hi] - r)
            kp = ks[h][:lo] * jnp.exp(r - bs[h][:lo])
            p[i, h] = lax.dot_general(qd.astype(BF16), kp.astype(BF16), nt, preferred_element_type=F32)
    off = {}
    for i in range(1, nblk):
        lo = i * HG_SUB
        for h in heads:
            off[i, h] = jnp.dot(p[i, h].astype(BF16), vs[h][:lo].astype(BF16), preferred_element_type=F32)

    row = lax.broadcasted_iota(jnp.int32, (HG_SUB, 1), 0)
    outs = []
    for h in heads:
        blocks = []
        for i in range(nblk):
            lo, hi = i * HG_SUB, (i + 1) * HG_SUB
            bi, qi, ki, vi = bs[h][lo:hi], qs[h][lo:hi], ks[h][lo:hi], vs[h][lo:hi]
            oi = o_in[h][lo:hi]
            if i > 0:
                oi = oi + off[i, h]
            for s in range(HG_SUB):
                e = jnp.exp(jnp.minimum(bi - bi[s:s + 1], 0.0))
                a = jnp.sum(qi * e * ki[s:s + 1], axis=-1, keepdims=True)
                a = jnp.where(row >= s, a, 0.0)
                oi = oi + a * vi[s:s + 1]
            blocks.append(oi)
        outs.append(jnp.concatenate(blocks, axis=0))
    return outs, st_new


def _hgrn_rec_kernel(q_ref, k_ref, v_ref, lf_ref, g_ref, nw_ref, o_ref, st_ref, *, n_chunks, dh):
    nh = st_ref.shape[0]

    @pl.when(pl.program_id(2) == 0)
    def _():
        st_ref[...] = jnp.zeros_like(st_ref)

    tri = _tri(HG_CHUNK)
    nw = nw_ref[...]

    def body(ci, carry):
        sl = pl.ds(pl.multiple_of(ci * HG_CHUNK, HG_CHUNK), HG_CHUNK)
        cols = [slice(h * dh, (h + 1) * dh) for h in range(nh)]
        outs, st_new = _hgrn_chunk([q_ref[0, sl, cs].astype(F32) for cs in cols],
                                   [k_ref[0, sl, cs].astype(F32) for cs in cols],
                                   [v_ref[0, sl, cs].astype(F32) for cs in cols], [lf_ref[0, sl, cs] for cs in cols],
                                   [st_ref[h] for h in range(nh)], tri)
        for h in range(nh):
            st_ref[h] = st_new[h]
            o = outs[h]
            o = o * lax.rsqrt(jnp.mean(o * o, axis=-1, keepdims=True) + RMS_EPS)
            o = o * nw * g_ref[0, sl, cols[h]].astype(F32)
            o_ref[0, sl, cols[h]] = o.astype(o_ref.dtype)
        return carry

    lax.fori_loop(0, n_chunks, body, 0)


def hgrn_recurrence(q, k, v, lf, g, norm_w, *, sblk=512, nhs=4):
    bsz, s, hd = q.shape
    dh = hd // HG_HEADS
    blk = pl.BlockSpec((1, sblk, nhs * dh), lambda b, h, t: (b, t, h))
    return pl.pallas_call(
        functools.partial(_hgrn_rec_kernel, n_chunks=sblk // HG_CHUNK, dh=dh),
        grid=(bsz, HG_HEADS // nhs, s // sblk),
        in_specs=[blk, blk, blk, blk, blk, pl.BlockSpec((1, dh), lambda b, h, t: (0, 0))],
        out_specs=blk,
        out_shape=jax.ShapeDtypeStruct((bsz, s, hd), BF16),
        scratch_shapes=[pltpu.VMEM((nhs, dh, dh), F32)],
        compiler_params=_params("parallel", "parallel", "arbitrary"),
        name="hgrn_recurrence",
    )(q, k, v, lf, g, norm_w)


def _mm_ln_kernel(h_ref, w_ref, x_ref, g_ref, b_ref, o_ref, *, alpha):
    y = jnp.dot(h_ref[...], w_ref[...], preferred_element_type=F32)
    o_ref[...] = _layer_norm(alpha * x_ref[...] + y, g_ref[...], b_ref[...])


def mm_ln(h, w, x2, g, b, *, alpha, tm=512):
    m, kdim = h.shape
    d = w.shape[1]
    row = lambda i: (i, 0)
    fixed = lambda i: (0, 0)
    return pl.pallas_call(
        functools.partial(_mm_ln_kernel, alpha=alpha),
        grid=(m // tm,),
        in_specs=[pl.BlockSpec((tm, kdim), row), pl.BlockSpec((kdim, d), fixed),
                  pl.BlockSpec((tm, d), row), pl.BlockSpec((1, d), fixed), pl.BlockSpec((1, d), fixed)],
        out_specs=pl.BlockSpec((tm, d), row),
        out_shape=jax.ShapeDtypeStruct((m, d), F32),
        compiler_params=_params("parallel"),
        name="mm_ln",
    )(h, w, x2, g, b)


def _ffn_ln_kernel(x_ref, wg_ref, wu_ref, wd_ref, g_ref, b_ref, o_ref, *, alpha):
    xb = x_ref[...].astype(BF16)
    a = jnp.dot(xb, wg_ref[...], preferred_element_type=F32)
    u = jnp.dot(xb, wu_ref[...], preferred_element_type=F32)
    y = jnp.dot((_silu(a) * u).astype(BF16), wd_ref[...], preferred_element_type=F32)
    o_ref[...] = _layer_norm(alpha * x_ref[...] + y, g_ref[...], b_ref[...])


def ffn_ln(x2, w_gu, w_down, g, b, *, alpha, tm=512):
    m, d = x2.shape
    f = w_down.shape[0]
    row = lambda i: (i, 0)
    fixed = lambda i: (0, 0)
    once = pl.Buffered(1)
    return pl.pallas_call(
        functools.partial(_ffn_ln_kernel, alpha=alpha),
        grid=(m // tm,),
        in_specs=[pl.BlockSpec((tm, d), row),
                  pl.BlockSpec((d, f), lambda i: (0, 0), pipeline_mode=once),
                  pl.BlockSpec((d, f), lambda i: (0, 1), pipeline_mode=once),
                  pl.BlockSpec((f, d), fixed, pipeline_mode=once),
                  pl.BlockSpec((1, d), fixed), pl.BlockSpec((1, d), fixed)],
        out_specs=pl.BlockSpec((tm, d), row),
        out_shape=jax.ShapeDtypeStruct((m, d), F32),
        compiler_params=_params("parallel"),
        name="ffn_ln",
    )(x2, w_gu, w_gu, w_down, g, b)


def _router_kernel(x_ref, wh_ref, wl_ref, o_ref):
    x = x_ref[...]
    xh = x.astype(BF16)
    xl = (x - xh.astype(F32)).astype(BF16)
    wh, wl = wh_ref[...], wl_ref[...]
    logits = (jnp.dot(xh, wh, preferred_element_type=F32)
              + (jnp.dot(xl, wh, preferred_element_type=F32) + jnp.dot(xh, wl, preferred_element_type=F32)))
    lane = lax.broadcasted_iota(jnp.int32, logits.shape, 1)
    logits = jnp.where(lane < N_EXPERTS, logits, -jnp.inf)
    m1 = jnp.max(logits, axis=-1, keepdims=True)
    i1 = jnp.min(jnp.where(logits == m1, lane, LANES), axis=-1, keepdims=True)
    rest = jnp.where(lane == i1, -jnp.inf, logits)
    m2 = jnp.max(rest, axis=-1, keepdims=True)
    i2 = jnp.min(jnp.where(rest == m2, lane, LANES), axis=-1, keepdims=True)
    e2 = jnp.exp(m2 - m1)
    den = 1.0 + e2
    w1 = 1.0 / den
    w2 = e2 / den
    o_ref[...] = jnp.where(lane == 0, i1.astype(F32),
                           jnp.where(lane == 1, i2.astype(F32),
                                     jnp.where(lane == 2, w1, jnp.where(lane == 3, w2, 0.0))))


def router(x2, w_hi, w_lo, *, tm=1024):
    m, d = x2.shape
    row = lambda i: (i, 0)
    fixed = lambda i: (0, 0)
    return pl.pallas_call(
        _router_kernel,
        grid=(m // tm,),
        in_specs=[pl.BlockSpec((tm, d), row), pl.BlockSpec((d, LANES), fixed), pl.BlockSpec((d, LANES), fixed)],
        out_specs=pl.BlockSpec((tm, LANES), row),
        out_shape=jax.ShapeDtypeStruct((m, LANES), F32),
        compiler_params=_params("parallel"),
        name="moe_router",
    )(x2, w_hi, w_lo)


def _row_copy(src_hbm, dst_vmem, sem, src_row, dst_row):
    return pltpu.make_async_copy(src_hbm.at[pl.ds(src_row, 1)], dst_vmem.at[pl.ds(dst_row, 1)], sem)


def _gather_start(idx_ref, base, src_hbm, dst_vmem, sem, n_rows):
    for r in range(n_rows):
        _row_copy(src_hbm, dst_vmem, sem, idx_ref[base + r], r).start()


def _gather_wait(src_hbm, dst_vmem, sem, n_rows):
    pltpu.make_async_copy(src_hbm.at[pl.ds(0, n_rows)], dst_vmem, sem).wait()


def _gather_kernel(idx_ref, src_hbm, o_ref, sem, *, rows):
    _gather_start(idx_ref, pl.program_id(0) * rows, src_hbm, o_ref, sem, rows)
    _gather_wait(src_hbm, o_ref, sem, rows)


def gather_rows(src, idx, *, rows=1024):
    n = idx.shape[0]
    d = src.shape[1]
    return pl.pallas_call(
        functools.partial(_gather_kernel, rows=rows),
        grid_spec=pltpu.PrefetchScalarGridSpec(
            num_scalar_prefetch=1,
            grid=(n // rows,),
            in_specs=[pl.BlockSpec(memory_space=pl.ANY)],
            out_specs=pl.BlockSpec((rows, d), lambda i, idx: (i, 0)),
            scratch_shapes=[pltpu.SemaphoreType.DMA(())]),
        out_shape=jax.ShapeDtypeStruct((n, d), src.dtype),
        compiler_params=_params("arbitrary"),
        name="moe_gather",
    )(idx, src)


def _experts_kernel(te_ref, nu_ref, xs_ref, wg_ref, wu_ref, wd_ref, o_ref, acc_ref):
    r, c = pl.program_id(0), pl.program_id(1)

    @pl.when(r < nu_ref[0])
    def _():
        xb = xs_ref[...].astype(BF16)
        a = jnp.dot(xb, wg_ref[0], preferred_element_type=F32)
        u = jnp.dot(xb, wu_ref[0], preferred_element_type=F32)
        part = jnp.dot((_silu(a) * u).astype(BF16), wd_ref[0], preferred_element_type=F32)

        @pl.when(c == 0)
        def _():
            acc_ref[...] = part

        @pl.when(c > 0)
        def _():
            acc_ref[...] += part

        @pl.when(c == pl.num_programs(1) - 1)
        def _():
            o_ref[...] = acc_ref[...]

    @pl.when(jnp.logical_and(r >= nu_ref[0], c == pl.num_programs(1) - 1))
    def _():
        o_ref[...] = jnp.zeros_like(o_ref)


def experts(xs, w_gu, w_down, tile_expert, n_used, *, tm, th=1792):
    n, d = xs.shape
    f = w_down.shape[1]
    nc = f // th
    return pl.pallas_call(
        _experts_kernel,
        grid_spec=pltpu.PrefetchScalarGridSpec(
            num_scalar_prefetch=2,
            grid=(n // tm, nc),
            in_specs=[pl.BlockSpec((tm, d), lambda r, c, te, nu: (r, 0)),
                      pl.BlockSpec((1, d, th), lambda r, c, te, nu: (te[r], 0, c)),
                      pl.BlockSpec((1, d, th), lambda r, c, te, nu: (te[r], 0, nc + c)),
                      pl.BlockSpec((1, th, d), lambda r, c, te, nu: (te[r], c, 0))],
            out_specs=pl.BlockSpec((tm, d), lambda r, c, te, nu: (r, 0)),
            scratch_shapes=[pltpu.VMEM((tm, d), F32)]),
        out_shape=jax.ShapeDtypeStruct((n, d), F32),
        compiler_params=_params("arbitrary", "arbitrary"),
        name="moe_experts",
    )(tile_expert, n_used, xs, w_gu, w_gu, w_down)


def _combine_ln_kernel(s1_ref, s2_ref, ys_hbm, x_ref, rw_ref, g_ref, b_ref, o_ref, a_buf, b_buf, sem, *,
                       alpha, rows):
    base = pl.program_id(0) * rows
    _gather_start(s1_ref, base, ys_hbm, a_buf, sem.at[0], rows)
    _gather_start(s2_ref, base, ys_hbm, b_buf, sem.at[1], rows)
    _gather_wait(ys_hbm, a_buf, sem.at[0], rows)
    _gather_wait(ys_hbm, b_buf, sem.at[1], rows)
    rw = rw_ref[...]
    y = rw[:, 2:3] * a_buf[...] + rw[:, 3:4] * b_buf[...]
    o_ref[...] = _layer_norm(alpha * x_ref[...] + y, g_ref[...], b_ref[...])


def combine_ln(ys, slot1, slot2, x2, rw, g, b, *, alpha, rows=512):
    m, d = x2.shape
    row = lambda i, s1, s2: (i, 0)
    fixed = lambda i, s1, s2: (0, 0)
    return pl.pallas_call(
        functools.partial(_combine_ln_kernel, alpha=alpha, rows=rows),
        grid_spec=pltpu.PrefetchScalarGridSpec(
            num_scalar_prefetch=2,
            grid=(m // rows,),
            in_specs=[pl.BlockSpec(memory_space=pl.ANY),
                      pl.BlockSpec((rows, d), row), pl.BlockSpec((rows, LANES), row),
                      pl.BlockSpec((1, d), fixed), pl.BlockSpec((1, d), fixed)],
            out_specs=pl.BlockSpec((rows, d), row),
            scratch_shapes=[pltpu.VMEM((rows, d), F32), pltpu.VMEM((rows, d), F32),
                            pltpu.SemaphoreType.DMA((2,))]),
        out_shape=jax.ShapeDtypeStruct((m, d), F32),
        compiler_params=_params("arbitrary"),
        name="moe_combine_ln",
    )(slot1, slot2, ys, x2, rw, g, b)


def moe_ln(x2, w_r_hi, w_r_lo, w_gu, w_down, g, b, *, alpha, tm=512):
    t = x2.shape[0]
    rw = router(x2, w_r_hi, w_r_lo)
    e_idx = rw[:, :TOP_K].astype(jnp.int32)
    flat_e = e_idx.reshape(-1)
    onehot = (flat_e[:, None] == jnp.arange(N_EXPERTS, dtype=jnp.int32)[None, :]).astype(jnp.int32)
    rank = jnp.take_along_axis(jnp.cumsum(onehot, axis=0) - onehot, flat_e[:, None], axis=1)[:, 0]
    counts = jnp.sum(onehot, axis=0)
    padded = ((counts + tm - 1) // tm) * tm
    ends = jnp.cumsum(padded)
    starts = ends - padded
    slot = starts[flat_e] + rank
    n_tiles = (TOP_K * t) // tm + N_EXPERTS
    n_slots = n_tiles * tm
    token = jnp.arange(TOP_K * t, dtype=jnp.int32) // TOP_K
    token_of_slot = jnp.zeros((n_slots,), jnp.int32).at[slot].set(token)
    tile_start = jnp.arange(n_tiles, dtype=jnp.int32) * tm
    tile_expert = jnp.minimum(jnp.sum((tile_start[:, None] >= ends[None, :]).astype(jnp.int32), axis=1),
                              N_EXPERTS - 1).astype(jnp.int32)
    n_used = (ends[-1:] // tm).astype(jnp.int32)

    xs = gather_rows(x2, token_of_slot)
    ys = experts(xs, w_gu, w_down, tile_expert, n_used, tm=tm)
    slot2d = slot.reshape(t, TOP_K).astype(jnp.int32)
    return combine_ln(ys, slot2d[:, 0], slot2d[:, 1], x2, rw, g, b, alpha=alpha)


def _mm_kernel(x_ref, w_ref, bias_ref, o_ref, *, kind, scale):
    acc = jnp.dot(x_ref[...].astype(BF16), w_ref[...], preferred_element_type=F32)
    if kind == "scale":
        acc = acc * scale
    elif kind == "sigmoid":
        acc = jax.nn.sigmoid(acc)
    elif kind == "log_sigmoid_bias":
        acc = _log_sigmoid(acc + bias_ref[...])
    o_ref[...] = acc.astype(o_ref.dtype)


def mm(x2, w, *, kind="none", scale=1.0, bias=None, out_dtype=BF16, tm=1024, tn=None):
    m, d = x2.shape
    n = w.shape[1]
    tn = n if tn is None else tn
    if bias is None:
        bias = jnp.zeros((1, n), F32)
    return pl.pallas_call(
        functools.partial(_mm_kernel, kind=kind, scale=scale),
        grid=(m // tm, n // tn),
        in_specs=[pl.BlockSpec((tm, d), lambda i, j: (i, 0)), pl.BlockSpec((d, tn), lambda i, j: (0, j)),
                  pl.BlockSpec((1, tn), lambda i, j: (0, j))],
        out_specs=pl.BlockSpec((tm, tn), lambda i, j: (i, j)),
        out_shape=jax.ShapeDtypeStruct((m, n), out_dtype),
        compiler_params=_params("parallel", "arbitrary"),
        name="mm_" + kind,
    )(x2, w, bias)


def _seq_cumsum_kernel(x_ref, o_ref, *, blk):
    tri = _tri(blk)
    n = x_ref.shape[1] // blk

    def body(i, carry):
        sl = pl.ds(pl.multiple_of(i * blk, blk), blk)
        c = _cumsum_rows(x_ref[0, sl, :], tri) + carry
        o_ref[0, sl, :] = c
        return c[blk - 1:blk]

    lax.fori_loop(0, n, body, jnp.zeros((1, x_ref.shape[2]), F32))


def seq_cumsum(x, *, blk=256):
    bsz, s, n = x.shape
    spec = pl.BlockSpec((1, s, n), lambda b: (b, 0, 0))
    return pl.pallas_call(
        functools.partial(_seq_cumsum_kernel, blk=blk),
        grid=(bsz,), in_specs=[spec], out_specs=spec,
        out_shape=jax.ShapeDtypeStruct(x.shape, F32),
        compiler_params=_params("parallel"),
        name="seq_cumsum",
    )(x)


def _aug_cols(c, width, first):
    hi, mid, lo = (t.astype(F32) for t in _split3(c))
    lane = lax.broadcasted_iota(jnp.int32, (c.shape[0], width), 1)
    fill = jnp.where(lane < 6, 1.0 if first == 0 else -1.0, 0.0)
    return jnp.where(lane == first, hi, jnp.where(lane == first + 1, mid, jnp.where(lane == first + 2, lo, fill)))


def _fox_kernel(q_ref, cq_ref, ka_ref, vt_ref, gate_ref, o_ref, acc_ref, qt_ref, *, tq, tk):
    qi = pl.program_id(2)
    nhs, dh = acc_ref.shape[0], acc_ref.shape[1]
    rowi = lax.broadcasted_iota(jnp.int32, (tk, tq), 0)
    coli = lax.broadcasted_iota(jnp.int32, (tk, tq), 1)
    acc_ref[...] = jnp.zeros_like(acc_ref)
    for j in range(nhs):
        qa = jnp.concatenate([q_ref[0, :, j * dh:(j + 1) * dh].astype(F32),
                              _aug_cols(cq_ref[0, 0, :, j:j + 1], dh, 3)], axis=1)
        qt_ref[j] = qa.T.astype(BF16)

    def tile(ki, stats, masked):
        ks = pl.ds(pl.multiple_of(ki * tk, tk), tk)
        heads = range(len(stats))
        ss = [jnp.dot(ka_ref[0, j, ks, :], qt_ref[j], preferred_element_type=F32) for j in heads]
        if masked:
            ss = [jnp.where(rowi <= coli, s, NEG_BIG) for s in ss]
        new, alphas, ps = [], [], []
        for j in heads:
            m_old, l_old = stats[j]
            m_new = jnp.maximum(m_old, jnp.max(ss[j], axis=0, keepdims=True))
            alpha = jnp.exp(m_old - m_new)
            p = jnp.exp(ss[j] - m_new)
            new.append((m_new, alpha * l_old + jnp.sum(p, axis=0, keepdims=True)))
            alphas.append(alpha)
            ps.append(p.astype(BF16))
        pvs = [jnp.dot(vt_ref[0, j, :, ks], ps[j], preferred_element_type=F32) for j in heads]
        for j in heads:
            acc_ref[j] = alphas[j] * acc_ref[j] + pvs[j]
        return tuple(new)

    init = tuple((jnp.full((1, tq), NEG_BIG, F32), jnp.zeros((1, tq), F32)) for _ in range(nhs))
    stats = lax.fori_loop(0, qi, lambda ki, st: tile(ki, st, False), init)
    stats = tile(qi, stats, True)
    ot = jnp.concatenate([acc_ref[j] / stats[j][1] for j in range(nhs)], axis=0)
    o_ref[0] = (ot.T * gate_ref[0].astype(F32)).astype(o_ref.dtype)


def fox_attention(q, cq, ka, vt, gate, *, tq=256, nhs=8):
    bsz, nh, dh, s = vt.shape
    aug = ka.shape[3]
    hp = nh // nhs
    row = pl.BlockSpec((1, tq, nhs * dh), lambda b, h, t: (b, t, h))
    return pl.pallas_call(
        functools.partial(_fox_kernel, tq=tq, tk=tq),
        grid=(bsz, hp, s // tq),
        in_specs=[row,
                  pl.BlockSpec((1, 1, tq, nhs), lambda b, h, t: (b, h, t, 0)),
                  pl.BlockSpec((1, nhs, s, aug), lambda b, h, t: (b, h, 0, 0)),
                  pl.BlockSpec((1, nhs, dh, s), lambda b, h, t: (b, h, 0, 0)),
                  row],
        out_specs=row,
        out_shape=jax.ShapeDtypeStruct((bsz, s, nh * dh), BF16),
        scratch_shapes=[pltpu.VMEM((nhs, dh, tq), F32), pltpu.VMEM((nhs, aug, tq), BF16)],
        compiler_params=_params("parallel", "parallel", "arbitrary"),
        name="fox_attention",
    )(q, cq, ka, vt, gate)


def _fox_prep_kernel(k_ref, v_ref, c_ref, ka_ref, vt_ref):
    nh, dh = vt_ref.shape[1], vt_ref.shape[2]
    c = c_ref[0]
    for h in range(nh):
        aug = _aug_cols(c[:, h:h + 1], ka_ref.shape[3] - dh, 0)
        ka_ref[0, h] = jnp.concatenate([k_ref[0, :, h * dh:(h + 1) * dh].astype(F32), aug], axis=1).astype(BF16)
    per = LANES // dh
    for g in range(nh // per):
        vt2 = v_ref[0, :, g * LANES:(g + 1) * LANES].astype(F32).T
        for j in range(per):
            vt_ref[0, g * per + j] = vt2[j * dh:(j + 1) * dh].astype(BF16)


def fox_prep_kv(k_sh, v_sh, c, *, ts=256):
    bsz, s, hd = k_sh.shape
    nh, dh = FOX_HEADS, FOX_HEAD_DIM
    row = pl.BlockSpec((1, ts, hd), lambda b, t: (b, t, 0))
    return pl.pallas_call(
        _fox_prep_kernel,
        grid=(bsz, s // ts),
        in_specs=[row, row, pl.BlockSpec((1, ts, LANES), lambda b, t: (b, t, 0))],
        out_specs=[pl.BlockSpec((1, nh, ts, LANES), lambda b, t: (b, 0, t, 0)),
                   pl.BlockSpec((1, nh, dh, ts), lambda b, t: (b, 0, 0, t))],
        out_shape=[jax.ShapeDtypeStruct((bsz, nh, s, LANES), BF16),
                   jax.ShapeDtypeStruct((bsz, nh, dh, s), BF16)],
        compiler_params=_params("parallel", "parallel"),
        name="fox_prep_kv",
    )(k_sh, v_sh, c)


def kernel(x, hg_w_in, hg_lb_logits, hg_norm_w, hg_w_out, kv_w, kv_fgate_b, fox_w_qg, fox_w_out, ln_mix_g, ln_mix_b, ln_ffn_g, ln_ffn_b, ffn_w_gu, ffn_w_down, moe_w_router, moe_w_gu, moe_w_down):
    bsz, s, d = x.shape
    depth = ln_mix_g.shape[0]
    n_a = hg_w_in.shape[0]
    alpha = (2 * depth) ** 0.25
    fox_dim = FOX_HEADS * FOX_HEAD_DIM
    t = bsz * s

    p = jax.nn.softmax(hg_lb_logits.astype(F32), axis=0)
    lbs = jnp.cumsum(p, axis=0) - p[0]

    def vec(a):
        return a.reshape(1, -1).astype(F32)

    x2 = x.reshape(t, d)
    ka = vt = cq = None
    for layer in range(depth):
        if layer < n_a:
            q, kk, v, lf, g = hgrn_proj(x2, hg_w_in[layer].astype(BF16), vec(lbs[layer]))
            sh = (bsz, s, -1)
            o = hgrn_recurrence(q.reshape(sh), kk.reshape(sh), v.reshape(sh), lf.reshape(sh), g.reshape(sh),
                                vec(hg_norm_w[layer]))
            x2 = mm_ln(o.reshape(t, -1), hg_w_out[layer].astype(BF16), x2, vec(ln_mix_g[layer]),
                       vec(ln_mix_b[layer]), alpha=alpha)
        else:
            bi = layer - n_a
            w_qg = fox_w_qg[bi].astype(BF16)
            q = mm(x2, w_qg[:, :fox_dim], kind="scale", scale=FOX_HEAD_DIM ** -0.5)
            gate = mm(x2, w_qg[:, fox_dim:], kind="sigmoid")
            sh = (bsz, s, fox_dim)
            o = fox_attention(q.reshape(sh), cq, ka, vt, gate.reshape(sh), nhs=FOX_HEADS_PER_STEP)
            x2 = mm_ln(o.reshape(t, fox_dim), fox_w_out[bi].astype(BF16), x2, vec(ln_mix_g[layer]),
                       vec(ln_mix_b[layer]), alpha=alpha)
        if layer % 2 == 0:
            x2 = ffn_ln(x2, ffn_w_gu[layer // 2].astype(BF16), ffn_w_down[layer // 2].astype(BF16),
                        vec(ln_ffn_g[layer]), vec(ln_ffn_b[layer]), alpha=alpha)
        else:
            w_r = jnp.pad(moe_w_router[layer // 2].astype(F32), ((0, 0), (0, LANES - N_EXPERTS)))
            w_r_hi = w_r.astype(BF16)
            w_r_lo = (w_r - w_r_hi.astype(F32)).astype(BF16)
            x2 = moe_ln(x2, w_r_hi, w_r_lo, moe_w_gu[layer // 2].astype(BF16), moe_w_down[layer // 2].astype(BF16),
                        vec(ln_ffn_g[layer]), vec(ln_ffn_b[layer]), alpha=alpha)
        if layer == n_a - 1:
            w_kv = kv_w.astype(BF16)
            k_sh = mm(x2, w_kv[:, :fox_dim]).reshape(bsz, s, fox_dim)
            v_sh = mm(x2, w_kv[:, fox_dim:2 * fox_dim]).reshape(bsz, s, fox_dim)
            w_f = jnp.pad(w_kv[:, 2 * fox_dim:], ((0, 0), (0, LANES - FOX_HEADS)))
            b_f = jnp.pad(kv_fgate_b.astype(F32), (0, LANES - FOX_HEADS)).reshape(1, LANES)
            log_f = mm(x2, w_f, kind="log_sigmoid_bias", bias=b_f, out_dtype=F32)
            c = seq_cumsum(log_f.reshape(bsz, s, LANES))
            ka, vt = fox_prep_kv(k_sh, v_sh, c)
            cq = c[:, :, :FOX_HEADS].reshape(bsz, s, FOX_HEADS // FOX_HEADS_PER_STEP, FOX_HEADS_PER_STEP)
            cq = cq.transpose(0, 2, 1, 3)
    return x2.reshape(bsz, s, d)
```

```python
import functools

import jax
import jax.numpy as jnp
from jax import lax
from jax.experimental import pallas as pl
from jax.experimental.pallas import tpu as pltpu

F32 = jnp.float32
BF16 = jnp.bfloat16

HG_HEADS = 8
HG_CHUNK = 64
HG_SUB = 16
LB_FLOOR = 1e-30
FOX_HEADS = 16
FOX_HEAD_DIM = 64
FOX_HEADS_PER_STEP = 8
N_EXPERTS = 8
TOP_K = 2
LN_EPS = 1e-5
RMS_EPS = 1e-6

LANES = 128
VMEM_LIMIT_BYTES = 56 * 1024 * 1024

NEG_BIG = -1e30


def _params(*sem):
    return pltpu.CompilerParams(dimension_semantics=sem, vmem_limit_bytes=VMEM_LIMIT_BYTES)


def _layer_norm(h, g, b):
    mu = jnp.mean(h, axis=-1, keepdims=True)
    d = h - mu
    var = jnp.mean(d * d, axis=-1, keepdims=True)
    return d * lax.rsqrt(var + LN_EPS) * g + b


def _silu(x):
    return x * jax.nn.sigmoid(x)


def _log1p_exp_neg(d):
    return jnp.log(1.0 + jnp.exp(-d))


def _log_sigmoid(z):
    return jnp.minimum(z, 0.0) - _log1p_exp_neg(jnp.abs(z))


def _split3(x):
    hi = x.astype(BF16)
    r1 = x - hi.astype(F32)
    mid = r1.astype(BF16)
    lo = (r1 - mid.astype(F32)).astype(BF16)
    return hi, mid, lo


def _tri(n):
    r = lax.broadcasted_iota(jnp.int32, (n, n), 0)
    c = lax.broadcasted_iota(jnp.int32, (n, n), 1)
    return (r >= c).astype(BF16)


def _cumsum_rows(x, tri):
    hi, mid, lo = _split3(x)
    return (jnp.dot(tri, hi, preferred_element_type=F32)
            + jnp.dot(tri, mid, preferred_element_type=F32)
            + jnp.dot(tri, lo, preferred_element_type=F32))


def _hgrn_proj_kernel(x_ref, w_ref, lb_ref, q_ref, k_ref, v_ref, lf_ref, g_ref):
    j = pl.program_id(1)
    n = q_ref.shape[1]
    w = w_ref[:, pl.ds(pl.multiple_of(j * n, n), n)]
    acc = jnp.dot(x_ref[...].astype(BF16), w, preferred_element_type=F32)

    @pl.when(j == 0)
    def _():
        q_ref[...] = _silu(acc).astype(q_ref.dtype)

    @pl.when(j == 1)
    def _():
        lb = lb_ref[...]
        log_lb = jnp.log(jnp.maximum(lb, LB_FLOOR))
        a = _log_sigmoid(acc)
        b = log_lb + _log_sigmoid(-acc)
        lf_ref[...] = jnp.maximum(a, b) + _log1p_exp_neg(jnp.abs(a - b))
        k_ref[...] = ((1.0 - lb) * jax.nn.sigmoid(-acc)).astype(k_ref.dtype)

    @pl.when(j == 2)
    def _():
        v_ref[...] = acc.astype(v_ref.dtype)

    @pl.when(j == 3)
    def _():
        g_ref[...] = _silu(acc).astype(g_ref.dtype)


def hgrn_proj(x2, w_in, lb, *, tm=512):
    m, d = x2.shape
    n = w_in.shape[1] // 4
    row = lambda i, j: (i, 0)
    fixed = lambda i, j: (0, 0)
    out_sd = jax.ShapeDtypeStruct((m, n), F32)
    out_bf = jax.ShapeDtypeStruct((m, n), BF16)
    return pl.pallas_call(
        _hgrn_proj_kernel,
        grid=(m // tm, 4),
        in_specs=[pl.BlockSpec((tm, d), row),
                  pl.BlockSpec((d, 4 * n), fixed, pipeline_mode=pl.Buffered(1)),
                  pl.BlockSpec((1, n), fixed)],
        out_specs=[pl.BlockSpec((tm, n), row)] * 5,
        out_shape=[out_bf, out_bf, out_bf, out_sd, out_bf],
        compiler_params=_params("parallel", "arbitrary"),
        name="hgrn_proj",
    )(x2, w_in, lb)


def _hgrn_chunk(qs, ks, vs, lfs, sts, tri):
    heads = range(len(qs))
    c = qs[0].shape[0]
    nblk = c // HG_SUB
    nt = (((1,), (1,)), ((), ()))
    tn = (((0,), (0,)), ((), ()))
    bs = [_cumsum_rows(lfs[h], tri) for h in heads]
    o_in = [lax.dot_general((qs[h] * jnp.exp(bs[h])).astype(BF16), sts[h].astype(BF16), nt,
                            preferred_element_type=F32) for h in heads]
    b_last = [bs[h][c - 1:c] for h in heads]
    upd = [lax.dot_general(vs[h].astype(BF16), (ks[h] * jnp.exp(b_last[h] - bs[h])).astype(BF16), tn,
                           preferred_element_type=F32) for h in heads]
    st_new = [sts[h] * jnp.exp(b_last[h]) + upd[h] for h in heads]

    p = {}
    for i in range(1, nblk):
        lo, hi = i * HG_SUB, (i + 1) * HG_SUB
        for h in heads:
            r = bs[h][lo - 1:lo]
            qd = qs[h][lo:hi] * jnp.exp(bs[h][lo:hi] - r)
            kp = ks[h][:lo] * jnp.exp(r - bs[h][:lo])
            p[i, h] = lax.dot_general(qd.astype(BF16), kp.astype(BF16), nt, preferred_element_type=F32)
    off = {}
    for i in range(1, nblk):
        lo = i * HG_SUB
        for h in heads:
            off[i, h] = jnp.dot(p[i, h].astype(BF16), vs[h][:lo].astype(BF16), preferred_element_type=F32)

    row = lax.broadcasted_iota(jnp.int32, (HG_SUB, 1), 0)
    outs = []
    for h in heads:
        blocks = []
        for i in range(nblk):
            lo, hi = i * HG_SUB, (i + 1) * HG_SUB
            bi, qi, ki, vi = bs[h][lo:hi], qs[h][lo:hi], ks[h][lo:hi], vs[h][lo:hi]
            oi = o_in[h][lo:hi]
            if i > 0:
                oi = oi + off[i, h]
            half = HG_SUB // 2
            o_hi = jnp.zeros((half, oi.shape[1]), F32)
            for s in range(HG_SUB):
                r0 = 0 if s < half else half
                e = jnp.exp(jnp.minimum(bi[r0:] - bi[s:s + 1], 0.0))
                a = jnp.sum(qi[r0:] * e * ki[s:s + 1], axis=-1, keepdims=True)
                a = jnp.where(row[r0:] >= s, a, 0.0)
                if r0 == 0:
                    oi = oi + a * vi[s:s + 1]
                else:
                    o_hi = o_hi + a * vi[s:s + 1]
            blocks.append(jnp.concatenate([oi[:half], oi[half:] + o_hi], axis=0))
        outs.append(jnp.concatenate(blocks, axis=0))
    return outs, st_new


def _hgrn_rec_kernel(q_ref, k_ref, v_ref, lf_ref, g_ref, nw_ref, o_ref, st_ref, *, n_chunks, dh):
    nh = st_ref.shape[0]

    @pl.when(pl.program_id(2) == 0)
    def _():
        st_ref[...] = jnp.zeros_like(st_ref)

    tri = _tri(HG_CHUNK)
    nw = nw_ref[...]

    def body(ci, carry):
        sl = pl.ds(pl.multiple_of(ci * HG_CHUNK, HG_CHUNK), HG_CHUNK)
        cols = [slice(h * dh, (h + 1) * dh) for h in range(nh)]
        outs, st_new = _hgrn_chunk([q_ref[0, sl, cs].astype(F32) for cs in cols],
                                   [k_ref[0, sl, cs].astype(F32) for cs in cols],
                                   [v_ref[0, sl, cs].astype(F32) for cs in cols], [lf_ref[0, sl, cs] for cs in cols],
                                   [st_ref[h] for h in range(nh)], tri)
        for h in range(nh):
            st_ref[h] = st_new[h]
            o = outs[h]
            o = o * lax.rsqrt(jnp.mean(o * o, axis=-1, keepdims=True) + RMS_EPS)
            o = o * nw * g_ref[0, sl, cols[h]].astype(F32)
            o_ref[0, sl, cols[h]] = o.astype(o_ref.dtype)
        return carry

    lax.fori_loop(0, n_chunks, body, 0)


def hgrn_recurrence(q, k, v, lf, g, norm_w, *, sblk=512, nhs=4):
    bsz, s, hd = q.shape
    dh = hd // HG_HEADS
    blk = pl.BlockSpec((1, sblk, nhs * dh), lambda b, h, t: (b, t, h))
    return pl.pallas_call(
        functools.partial(_hgrn_rec_kernel, n_chunks=sblk // HG_CHUNK, dh=dh),
        grid=(bsz, HG_HEADS // nhs, s // sblk),
        in_specs=[blk, blk, blk, blk, blk, pl.BlockSpec((1, dh), lambda b, h, t: (0, 0))],
        out_specs=blk,
        out_shape=jax.ShapeDtypeStruct((bsz, s, hd), BF16),
        scratch_shapes=[pltpu.VMEM((nhs, dh, dh), F32)],
        compiler_params=_params("parallel", "parallel", "arbitrary"),
        name="hgrn_recurrence",
    )(q, k, v, lf, g, norm_w)


def _mm_ln_kernel(h_ref, w_ref, x_ref, g_ref, b_ref, o_ref, *, alpha):
    y = jnp.dot(h_ref[...], w_ref[...], preferred_element_type=F32)
    o_ref[...] = _layer_norm(alpha * x_ref[...] + y, g_ref[...], b_ref[...])


def mm_ln(h, w, x2, g, b, *, alpha, tm=512):
    m, kdim = h.shape
    d = w.shape[1]
    row = lambda i: (i, 0)
    fixed = lambda i: (0, 0)
    return pl.pallas_call(
        functools.partial(_mm_ln_kernel, alpha=alpha),
        grid=(m // tm,),
        in_specs=[pl.BlockSpec((tm, kdim), row), pl.BlockSpec((kdim, d), fixed),
                  pl.BlockSpec((tm, d), row), pl.BlockSpec((1, d), fixed), pl.BlockSpec((1, d), fixed)],
        out_specs=pl.BlockSpec((tm, d), row),
        out_shape=jax.ShapeDtypeStruct((m, d), F32),
        compiler_params=_params("parallel"),
        name="mm_ln",
    )(h, w, x2, g, b)


def _ffn_ln_kernel(x_ref, wg_ref, wu_ref, wd_ref, g_ref, b_ref, o_ref, *, alpha):
    xb = x_ref[...].astype(BF16)
    a = jnp.dot(xb, wg_ref[...], preferred_element_type=F32)
    u = jnp.dot(xb, wu_ref[...], preferred_element_type=F32)
    y = jnp.dot((_silu(a) * u).astype(BF16), wd_ref[...], preferred_element_type=F32)
    o_ref[...] = _layer_norm(alpha * x_ref[...] + y, g_ref[...], b_ref[...])


def ffn_ln(x2, w_gu, w_down, g, b, *, alpha, tm=512):
    m, d = x2.shape
    f = w_down.shape[0]
    row = lambda i: (i, 0)
    fixed = lambda i: (0, 0)
    once = pl.Buffered(1)
    return pl.pallas_call(
        functools.partial(_ffn_ln_kernel, alpha=alpha),
        grid=(m // tm,),
        in_specs=[pl.BlockSpec((tm, d), row),
                  pl.BlockSpec((d, f), lambda i: (0, 0), pipeline_mode=once),
                  pl.BlockSpec((d, f), lambda i: (0, 1), pipeline_mode=once),
                  pl.BlockSpec((f, d), fixed, pipeline_mode=once),
                  pl.BlockSpec((1, d), fixed), pl.BlockSpec((1, d), fixed)],
        out_specs=pl.BlockSpec((tm, d), row),
        out_shape=jax.ShapeDtypeStruct((m, d), F32),
        compiler_params=_params("parallel"),
        name="ffn_ln",
    )(x2, w_gu, w_gu, w_down, g, b)


def _router_kernel(x_ref, wh_ref, wl_ref, o_ref):
    x = x_ref[...]
    xh = x.astype(BF16)
    xl = (x - xh.astype(F32)).astype(BF16)
    wh, wl = wh_ref[...], wl_ref[...]
    logits = (jnp.dot(xh, wh, preferred_element_type=F32)
              + (jnp.dot(xl, wh, preferred_element_type=F32) + jnp.dot(xh, wl, preferred_element_type=F32)))
    lane = lax.broadcasted_iota(jnp.int32, logits.shape, 1)
    logits = jnp.where(lane < N_EXPERTS, logits, -jnp.inf)
    m1 = jnp.max(logits, axis=-1, keepdims=True)
    i1 = jnp.min(jnp.where(logits == m1, lane, LANES), axis=-1, keepdims=True)
    rest = jnp.where(lane == i1, -jnp.inf, logits)
    m2 = jnp.max(rest, axis=-1, keepdims=True)
    i2 = jnp.min(jnp.where(rest == m2, lane, LANES), axis=-1, keepdims=True)
    e2 = jnp.exp(m2 - m1)
    den = 1.0 + e2
    w1 = 1.0 / den
    w2 = e2 / den
    o_ref[...] = jnp.where(lane == 0, i1.astype(F32),
                           jnp.where(lane == 1, i2.astype(F32),
                                     jnp.where(lane == 2, w1, jnp.where(lane == 3, w2, 0.0))))


def router(x2, w_hi, w_lo, *, tm=1024):
    m, d = x2.shape
    row = lambda i: (i, 0)
    fixed = lambda i: (0, 0)
    return pl.pallas_call(
        _router_kernel,
        grid=(m // tm,),
        in_specs=[pl.BlockSpec((tm, d), row), pl.BlockSpec((d, LANES), fixed), pl.BlockSpec((d, LANES), fixed)],
        out_specs=pl.BlockSpec((tm, LANES), row),
        out_shape=jax.ShapeDtypeStruct((m, LANES), F32),
        compiler_params=_params("parallel"),
        name="moe_router",
    )(x2, w_hi, w_lo)


def _row_copy(src_hbm, dst_vmem, sem, src_row, dst_row):
    return pltpu.make_async_copy(src_hbm.at[pl.ds(src_row, 1)], dst_vmem.at[pl.ds(dst_row, 1)], sem)


def _gather_start(idx_ref, base, src_hbm, dst_vmem, sem, n_rows):
    for r in range(n_rows):
        _row_copy(src_hbm, dst_vmem, sem, idx_ref[base + r], r).start(priority=r % 2)


def _gather_wait(src_hbm, dst_vmem, sem, n_rows):
    pltpu.make_async_copy(src_hbm.at[pl.ds(0, n_rows)], dst_vmem, sem).wait()


def _gather_kernel(idx_ref, src_hbm, o_ref, sem, *, rows):
    _gather_start(idx_ref, pl.program_id(0) * rows, src_hbm, o_ref, sem, rows)
    _gather_wait(src_hbm, o_ref, sem, rows)


def gather_rows(src, idx, *, rows=1024):
    n = idx.shape[0]
    d = src.shape[1]
    return pl.pallas_call(
        functools.partial(_gather_kernel, rows=rows),
        grid_spec=pltpu.PrefetchScalarGridSpec(
            num_scalar_prefetch=1,
            grid=(n // rows,),
            in_specs=[pl.BlockSpec(memory_space=pl.ANY)],
            out_specs=pl.BlockSpec((rows, d), lambda i, idx: (i, 0)),
            scratch_shapes=[pltpu.SemaphoreType.DMA(())]),
        out_shape=jax.ShapeDtypeStruct((n, d), src.dtype),
        compiler_params=_params("arbitrary"),
        name="moe_gather",
    )(idx, src)


def _experts_kernel(te_ref, nu_ref, xs_ref, wg_ref, wu_ref, wd_ref, o_ref, acc_ref):
    r, c = pl.program_id(0), pl.program_id(1)

    @pl.when(r < nu_ref[0])
    def _():
        xb = xs_ref[...].astype(BF16)
        a = jnp.dot(xb, wg_ref[0], preferred_element_type=F32)
        u = jnp.dot(xb, wu_ref[0], preferred_element_type=F32)
        part = jnp.dot((_silu(a) * u).astype(BF16), wd_ref[0], preferred_element_type=F32)

        @pl.when(c == 0)
        def _():
            acc_ref[...] = part

        @pl.when(c > 0)
        def _():
            acc_ref[...] += part

        @pl.when(c == pl.num_programs(1) - 1)
        def _():
            o_ref[...] = acc_ref[...]

    @pl.when(jnp.logical_and(r >= nu_ref[0], c == pl.num_programs(1) - 1))
    def _():
        o_ref[...] = jnp.zeros_like(o_ref)


def experts(xs, w_gu, w_down, tile_expert, n_used, *, tm, th=1792):
    n, d = xs.shape
    f = w_down.shape[1]
    nc = f // th
    return pl.pallas_call(
        _experts_kernel,
        grid_spec=pltpu.PrefetchScalarGridSpec(
            num_scalar_prefetch=2,
            grid=(n // tm, nc),
            in_specs=[pl.BlockSpec((tm, d), lambda r, c, te, nu: (r, 0)),
                      pl.BlockSpec((1, d, th), lambda r, c, te, nu: (te[r], 0, c)),
                      pl.BlockSpec((1, d, th), lambda r, c, te, nu: (te[r], 0, nc + c)),
                      pl.BlockSpec((1, th, d), lambda r, c, te, nu: (te[r], c, 0))],
            out_specs=pl.BlockSpec((tm, d), lambda r, c, te, nu: (r, 0)),
            scratch_shapes=[pltpu.VMEM((tm, d), F32)]),
        out_shape=jax.ShapeDtypeStruct((n, d), F32),
        compiler_params=_params("arbitrary", "arbitrary"),
        name="moe_experts",
    )(tile_expert, n_used, xs, w_gu, w_gu, w_down)


def _combine_ln_kernel(s1_ref, s2_ref, ys_hbm, x_ref, rw_ref, g_ref, b_ref, o_ref, a_buf, b_buf, sem, *,
                       alpha, rows):
    base = pl.program_id(0) * rows
    _gather_start(s1_ref, base, ys_hbm, a_buf, sem.at[0], rows)
    _gather_start(s2_ref, base, ys_hbm, b_buf, sem.at[1], rows)
    _gather_wait(ys_hbm, a_buf, sem.at[0], rows)
    _gather_wait(ys_hbm, b_buf, sem.at[1], rows)
    rw = rw_ref[...]
    y = rw[:, 2:3] * a_buf[...] + rw[:, 3:4] * b_buf[...]
    o_ref[...] = _layer_norm(alpha * x_ref[...] + y, g_ref[...], b_ref[...])


def combine_ln(ys, slot1, slot2, x2, rw, g, b, *, alpha, rows=512):
    m, d = x2.shape
    row = lambda i, s1, s2: (i, 0)
    fixed = lambda i, s1, s2: (0, 0)
    return pl.pallas_call(
        functools.partial(_combine_ln_kernel, alpha=alpha, rows=rows),
        grid_spec=pltpu.PrefetchScalarGridSpec(
            num_scalar_prefetch=2,
            grid=(m // rows,),
            in_specs=[pl.BlockSpec(memory_space=pl.ANY),
                      pl.BlockSpec((rows, d), row), pl.BlockSpec((rows, LANES), row),
                      pl.BlockSpec((1, d), fixed), pl.BlockSpec((1, d), fixed)],
            out_specs=pl.BlockSpec((rows, d), row),
            scratch_shapes=[pltpu.VMEM((rows, d), F32), pltpu.VMEM((rows, d), F32),
                            pltpu.SemaphoreType.DMA((2,))]),
        out_shape=jax.ShapeDtypeStruct((m, d), F32),
        compiler_params=_params("arbitrary"),
        name="moe_combine_ln",
    )(slot1, slot2, ys, x2, rw, g, b)


def moe_ln(x2, w_r_hi, w_r_lo, w_gu, w_down, g, b, *, alpha, tm=512):
    t = x2.shape[0]
    rw = router(x2, w_r_hi, w_r_lo)
    e_idx = rw[:, :TOP_K].astype(jnp.int32)
    flat_e = e_idx.reshape(-1)
    onehot = (flat_e[:, None] == jnp.arange(N_EXPERTS, dtype=jnp.int32)[None, :]).astype(jnp.int32)
    rank = jnp.take_along_axis(jnp.cumsum(onehot, axis=0) - onehot, flat_e[:, None], axis=1)[:, 0]
    counts = jnp.sum(onehot, axis=0)
    padded = ((counts + tm - 1) // tm) * tm
    ends = jnp.cumsum(padded)
    starts = ends - padded
    slot = starts[flat_e] + rank
    n_tiles = (TOP_K * t) // tm + N_EXPERTS
    n_slots = n_tiles * tm
    token = jnp.arange(TOP_K * t, dtype=jnp.int32) // TOP_K
    token_of_slot = jnp.zeros((n_slots,), jnp.int32).at[slot].set(token)
    tile_start = jnp.arange(n_tiles, dtype=jnp.int32) * tm
    tile_expert = jnp.minimum(jnp.sum((tile_start[:, None] >= ends[None, :]).astype(jnp.int32), axis=1),
                              N_EXPERTS - 1).astype(jnp.int32)
    n_used = (ends[-1:] // tm).astype(jnp.int32)

    xs = gather_rows(x2, token_of_slot)
    ys = experts(xs, w_gu, w_down, tile_expert, n_used, tm=tm)
    slot2d = slot.reshape(t, TOP_K).astype(jnp.int32)
    return combine_ln(ys, slot2d[:, 0], slot2d[:, 1], x2, rw, g, b, alpha=alpha)


def _mm_kernel(x_ref, w_ref, bias_ref, o_ref, *, kind, scale):
    acc = jnp.dot(x_ref[...].astype(BF16), w_ref[...], preferred_element_type=F32)
    if kind == "scale":
        acc = acc * scale
    elif kind == "sigmoid":
        acc = jax.nn.sigmoid(acc)
    elif kind == "log_sigmoid_bias":
        acc = _log_sigmoid(acc + bias_ref[...])
    o_ref[...] = acc.astype(o_ref.dtype)


def mm(x2, w, *, kind="none", scale=1.0, bias=None, out_dtype=BF16, tm=1024, tn=None):
    m, d = x2.shape
    n = w.shape[1]
    tn = n if tn is None else tn
    if bias is None:
        bias = jnp.zeros((1, n), F32)
    return pl.pallas_call(
        functools.partial(_mm_kernel, kind=kind, scale=scale),
        grid=(m // tm, n // tn),
        in_specs=[pl.BlockSpec((tm, d), lambda i, j: (i, 0)), pl.BlockSpec((d, tn), lambda i, j: (0, j)),
                  pl.BlockSpec((1, tn), lambda i, j: (0, j))],
        out_specs=pl.BlockSpec((tm, tn), lambda i, j: (i, j)),
        out_shape=jax.ShapeDtypeStruct((m, n), out_dtype),
        compiler_params=_params("parallel", "arbitrary"),
        name="mm_" + kind,
    )(x2, w, bias)


def _seq_cumsum_kernel(x_ref, o_ref, *, blk):
    tri = _tri(blk)
    n = x_ref.shape[1] // blk

    def body(i, carry):
        sl = pl.ds(pl.multiple_of(i * blk, blk), blk)
        c = _cumsum_rows(x_ref[0, sl, :], tri) + carry
        o_ref[0, sl, :] = c
        return c[blk - 1:blk]

    lax.fori_loop(0, n, body, jnp.zeros((1, x_ref.shape[2]), F32))


def seq_cumsum(x, *, blk=256):
    bsz, s, n = x.shape
    spec = pl.BlockSpec((1, s, n), lambda b: (b, 0, 0))
    return pl.pallas_call(
        functools.partial(_seq_cumsum_kernel, blk=blk),
        grid=(bsz,), in_specs=[spec], out_specs=spec,
        out_shape=jax.ShapeDtypeStruct(x.shape, F32),
        compiler_params=_params("parallel"),
        name="seq_cumsum",
    )(x)


def _aug_cols(c, width, first):
    hi, mid, lo = (t.astype(F32) for t in _split3(c))
    lane = lax.broadcasted_iota(jnp.int32, (c.shape[0], width), 1)
    fill = jnp.where(lane < 6, 1.0 if first == 0 else -1.0, 0.0)
    return jnp.where(lane == first, hi, jnp.where(lane == first + 1, mid, jnp.where(lane == first + 2, lo, fill)))


def _fox_kernel(q_ref, cq_ref, ka_ref, vt_ref, gate_ref, o_ref, acc_ref, qt_ref, *, tq, tk):
    qi = pl.program_id(2)
    nhs, dh = acc_ref.shape[0], acc_ref.shape[1]
    rowi = lax.broadcasted_iota(jnp.int32, (tk, tq), 0)
    coli = lax.broadcasted_iota(jnp.int32, (tk, tq), 1)
    acc_ref[...] = jnp.zeros_like(acc_ref)
    for j in range(nhs):
        qa = jnp.concatenate([q_ref[0, :, j * dh:(j + 1) * dh].astype(F32),
                              _aug_cols(cq_ref[0, 0, :, j:j + 1], dh, 3)], axis=1)
        qt_ref[j] = qa.T.astype(BF16)

    def tile(ki, stats, masked):
        ks = pl.ds(pl.multiple_of(ki * tk, tk), tk)
        heads = range(len(stats))
        ss = [jnp.dot(ka_ref[0, j, ks, :], qt_ref[j], preferred_element_type=F32) for j in heads]
        if masked:
            ss = [jnp.where(rowi <= coli, s, NEG_BIG) for s in ss]
        new, alphas, ps = [], [], []
        for j in heads:
            m_old, l_old = stats[j]
            m_new = jnp.maximum(m_old, jnp.max(ss[j], axis=0, keepdims=True))
            alpha = jnp.exp(m_old - m_new)
            p = jnp.exp(ss[j] - m_new)
            new.append((m_new, alpha * l_old + jnp.sum(p, axis=0, keepdims=True)))
            alphas.append(alpha)
            ps.append(p.astype(BF16))
        pvs = [jnp.dot(vt_ref[0, j, :, ks], ps[j], preferred_element_type=F32) for j in heads]
        for j in heads:
            acc_ref[j] = alphas[j] * acc_ref[j] + pvs[j]
        return tuple(new)

    init = tuple((jnp.full((1, tq), NEG_BIG, F32), jnp.zeros((1, tq), F32)) for _ in range(nhs))
    stats = lax.fori_loop(0, qi, lambda ki, st: tile(ki, st, False), init)
    stats = tile(qi, stats, True)
    ot = jnp.concatenate([acc_ref[j] / stats[j][1] for j in range(nhs)], axis=0)
    o_ref[0] = (ot.T * gate_ref[0].astype(F32)).astype(o_ref.dtype)


def fox_attention(q, cq, ka, vt, gate, *, tq=256, nhs=8):
    bsz, nh, dh, s = vt.shape
    aug = ka.shape[3]
    hp = nh // nhs
    row = pl.BlockSpec((1, tq, nhs * dh), lambda b, h, t: (b, t, h))
    return pl.pallas_call(
        functools.partial(_fox_kernel, tq=tq, tk=tq),
        grid=(bsz, hp, s // tq),
        in_specs=[row,
                  pl.BlockSpec((1, 1, tq, nhs), lambda b, h, t: (b, h, t, 0)),
                  pl.BlockSpec((1, nhs, s, aug), lambda b, h, t: (b, h, 0, 0)),
                  pl.BlockSpec((1, nhs, dh, s), lambda b, h, t: (b, h, 0, 0)),
                  row],
        out_specs=row,
        out_shape=jax.ShapeDtypeStruct((bsz, s, nh * dh), BF16),
        scratch_shapes=[pltpu.VMEM((nhs, dh, tq), F32), pltpu.VMEM((nhs, aug, tq), BF16)],
        compiler_params=_params("parallel", "parallel", "arbitrary"),
        name="fox_attention",
    )(q, cq, ka, vt, gate)


def _fox_prep_kernel(k_ref, v_ref, c_ref, ka_ref, vt_ref):
    nh, dh = vt_ref.shape[1], vt_ref.shape[2]
    c = c_ref[0]
    for h in range(nh):
        aug = _aug_cols(c[:, h:h + 1], ka_ref.shape[3] - dh, 0)
        ka_ref[0, h] = jnp.concatenate([k_ref[0, :, h * dh:(h + 1) * dh].astype(F32), aug], axis=1).astype(BF16)
    per = LANES // dh
    for g in range(nh // per):
        vt2 = v_ref[0, :, g * LANES:(g + 1) * LANES].astype(F32).T
        for j in range(per):
            vt_ref[0, g * per + j] = vt2[j * dh:(j + 1) * dh].astype(BF16)


def fox_prep_kv(k_sh, v_sh, c, *, ts=256):
    bsz, s, hd = k_sh.shape
    nh, dh = FOX_HEADS, FOX_HEAD_DIM
    row = pl.BlockSpec((1, ts, hd), lambda b, t: (b, t, 0))
    return pl.pallas_call(
        _fox_prep_kernel,
        grid=(bsz, s // ts),
        in_specs=[row, row, pl.BlockSpec((1, ts, LANES), lambda b, t: (b, t, 0))],
        out_specs=[pl.BlockSpec((1, nh, ts, LANES), lambda b, t: (b, 0, t, 0)),
                   pl.BlockSpec((1, nh, dh, ts), lambda b, t: (b, 0, 0, t))],
        out_shape=[jax.ShapeDtypeStruct((bsz, nh, s, LANES), BF16),
                   jax.ShapeDtypeStruct((bsz, nh, dh, s), BF16)],
        compiler_params=_params("parallel", "parallel"),
        name="fox_prep_kv",
    )(k_sh, v_sh, c)


def kernel(x, hg_w_in, hg_lb_logits, hg_norm_w, hg_w_out, kv_w, kv_fgate_b, fox_w_qg, fox_w_out, ln_mix_g, ln_mix_b, ln_ffn_g, ln_ffn_b, ffn_w_gu, ffn_w_down, moe_w_router, moe_w_gu, moe_w_down):
    bsz, s, d = x.shape
    depth = ln_mix_g.shape[0]
    n_a = hg_w_in.shape[0]
    alpha = (2 * depth) ** 0.25
    fox_dim = FOX_HEADS * FOX_HEAD_DIM
    t = bsz * s

    p = jax.nn.softmax(hg_lb_logits.astype(F32), axis=0)
    lbs = jnp.cumsum(p, axis=0) - p[0]

    def vec(a):
        return a.reshape(1, -1).astype(F32)

    x2 = x.reshape(t, d)
    ka = vt = cq = None
    for layer in range(depth):
        if layer < n_a:
            q, kk, v, lf, g = hgrn_proj(x2, hg_w_in[layer].astype(BF16), vec(lbs[layer]))
            sh = (bsz, s, -1)
            o = hgrn_recurrence(q.reshape(sh), kk.reshape(sh), v.reshape(sh), lf.reshape(sh), g.reshape(sh),
                                vec(hg_norm_w[layer]))
            x2 = mm_ln(o.reshape(t, -1), hg_w_out[layer].astype(BF16), x2, vec(ln_mix_g[layer]),
                       vec(ln_mix_b[layer]), alpha=alpha)
        else:
            bi = layer - n_a
            w_qg = fox_w_qg[bi].astype(BF16)
            q = mm(x2, w_qg[:, :fox_dim], kind="scale", scale=FOX_HEAD_DIM ** -0.5)
            gate = mm(x2, w_qg[:, fox_dim:], kind="sigmoid")
            sh = (bsz, s, fox_dim)
            o = fox_attention(q.reshape(sh), cq, ka, vt, gate.reshape(sh), nhs=FOX_HEADS_PER_STEP)
            x2 = mm_ln(o.reshape(t, fox_dim), fox_w_out[bi].astype(BF16), x2, vec(ln_mix_g[layer]),
                       vec(ln_mix_b[layer]), alpha=alpha)
        if layer % 2 == 0:
            x2 = ffn_ln(x2, ffn_w_gu[layer // 2].astype(BF16), ffn_w_down[layer // 2].astype(BF16),
                        vec(ln_ffn_g[layer]), vec(ln_ffn_b[layer]), alpha=alpha)
        else:
            w_r = jnp.pad(moe_w_router[layer // 2].astype(F32), ((0, 0), (0, LANES - N_EXPERTS)))
            w_r_hi = w_r.astype(BF16)
            w_r_lo = (w_r - w_r_hi.astype(F32)).astype(BF16)
            x2 = moe_ln(x2, w_r_hi, w_r_lo, moe_w_gu[layer // 2].astype(BF16), moe_w_down[layer // 2].astype(BF16),
                        vec(ln_ffn_g[layer]), vec(ln_ffn_b[layer]), alpha=alpha)
        if layer == n_a - 1:
            w_kv = kv_w.astype(BF16)
            k_sh = mm(x2, w_kv[:, :fox_dim]).reshape(bsz, s, fox_dim)
            v_sh = mm(x2, w_kv[:, fox_dim:2 * fox_dim]).reshape(bsz, s, fox_dim)
            w_f = jnp.pad(w_kv[:, 2 * fox_dim:], ((0, 0), (0, LANES - FOX_HEADS)))
            b_f = jnp.pad(kv_fgate_b.astype(F32), (0, LANES - FOX_HEADS)).reshape(1, LANES)
            log_f = mm(x2, w_f, kind="log_sigmoid_bias", bias=b_f, out_dtype=F32)
            c = seq_cumsum(log_f.reshape(bsz, s, LANES))
            ka, vt = fox_prep_kv(k_sh, v_sh, c)
            cq = c[:, :, :FOX_HEADS].reshape(bsz, s, FOX_HEADS // FOX_HEADS_PER_STEP, FOX_HEADS_PER_STEP)
            cq = cq.transpose(0, 2, 1, 3)
    return x2.reshape(bsz, s, d)
```

```python
import functools

import jax
import jax.numpy as jnp
from jax import lax
from jax.experimental import pallas as pl
from jax.experimental.pallas import tpu as pltpu

F32 = jnp.float32
BF16 = jnp.bfloat16

HG_HEADS = 8
HG_CHUNK = 64
HG_SUB = 8
LB_FLOOR = 1e-30
FOX_HEADS = 16
FOX_HEAD_DIM = 64
FOX_HEADS_PER_STEP = 8
N_EXPERTS = 8
TOP_K = 2
LN_EPS = 1e-5
RMS_EPS = 1e-6

LANES = 128
VMEM_LIMIT_BYTES = 56 * 1024 * 1024

NEG_BIG = -1e30


def _params(*sem):
    return pltpu.CompilerParams(dimension_semantics=sem, vmem_limit_bytes=VMEM_LIMIT_BYTES)


def _layer_norm(h, g, b):
    mu = jnp.mean(h, axis=-1, keepdims=True)
    d = h - mu
    var = jnp.mean(d * d, axis=-1, keepdims=True)
    return d * lax.rsqrt(var + LN_EPS) * g + b


def _silu(x):
    return x * jax.nn.sigmoid(x)


def _log1p_exp_neg(d):
    return jnp.log(1.0 + jnp.exp(-d))


def _log_sigmoid(z):
    return jnp.minimum(z, 0.0) - _log1p_exp_neg(jnp.abs(z))


def _split3(x):
    hi = x.astype(BF16)
    r1 = x - hi.astype(F32)
    mid = r1.astype(BF16)
    lo = (r1 - mid.astype(F32)).astype(BF16)
    return hi, mid, lo


def _tri(n):
    r = lax.broadcasted_iota(jnp.int32, (n, n), 0)
    c = lax.broadcasted_iota(jnp.int32, (n, n), 1)
    return (r >= c).astype(BF16)


def _cumsum_rows(x, tri):
    hi, mid, lo = _split3(x)
    return (jnp.dot(tri, hi, preferred_element_type=F32)
            + jnp.dot(tri, mid, preferred_element_type=F32)
            + jnp.dot(tri, lo, preferred_element_type=F32))


def _hgrn_proj_kernel(x_ref, w_ref, lb_ref, q_ref, k_ref, v_ref, lf_ref, g_ref):
    j = pl.program_id(1)
    n = q_ref.shape[1]
    w = w_ref[:, pl.ds(pl.multiple_of(j * n, n), n)]
    acc = jnp.dot(x_ref[...].astype(BF16), w, preferred_element_type=F32)

    @pl.when(j == 0)
    def _():
        q_ref[...] = _silu(acc).astype(q_ref.dtype)

    @pl.when(j == 1)
    def _():
        lb = lb_ref[...]
        log_lb = jnp.log(jnp.maximum(lb, LB_FLOOR))
        a = _log_sigmoid(acc)
        b = log_lb + _log_sigmoid(-acc)
        lf_ref[...] = jnp.maximum(a, b) + _log1p_exp_neg(jnp.abs(a - b))
        k_ref[...] = ((1.0 - lb) * jax.nn.sigmoid(-acc)).astype(k_ref.dtype)

    @pl.when(j == 2)
    def _():
        v_ref[...] = acc.astype(v_ref.dtype)

    @pl.when(j == 3)
    def _():
        g_ref[...] = _silu(acc).astype(g_ref.dtype)


def hgrn_proj(x2, w_in, lb, *, tm=512):
    m, d = x2.shape
    n = w_in.shape[1] // 4
    row = lambda i, j: (i, 0)
    fixed = lambda i, j: (0, 0)
    out_sd = jax.ShapeDtypeStruct((m, n), F32)
    out_bf = jax.ShapeDtypeStruct((m, n), BF16)
    return pl.pallas_call(
        _hgrn_proj_kernel,
        grid=(m // tm, 4),
        in_specs=[pl.BlockSpec((tm, d), row),
                  pl.BlockSpec((d, 4 * n), fixed, pipeline_mode=pl.Buffered(1)),
                  pl.BlockSpec((1, n), fixed)],
        out_specs=[pl.BlockSpec((tm, n), row)] * 5,
        out_shape=[out_bf, out_bf, out_bf, out_sd, out_bf],
        compiler_params=_params("parallel", "arbitrary"),
        name="hgrn_proj",
    )(x2, w_in, lb)


def _hgrn_chunk(qs, ks, vs, lfs, sts, tri):
    heads = range(len(qs))
    c = qs[0].shape[0]
    nblk = c // HG_SUB
    nt = (((1,), (1,)), ((), ()))
    tn = (((0,), (0,)), ((), ()))
    bs = [_cumsum_rows(lfs[h], tri) for h in heads]
    o_in = [lax.dot_general((qs[h] * jnp.exp(bs[h])).astype(BF16), sts[h].astype(BF16), nt,
                            preferred_element_type=F32) for h in heads]
    b_last = [bs[h][c - 1:c] for h in heads]
    upd = [lax.dot_general(vs[h].astype(BF16), (ks[h] * jnp.exp(b_last[h] - bs[h])).astype(BF16), tn,
                           preferred_element_type=F32) for h in heads]
    st_new = [sts[h] * jnp.exp(b_last[h]) + upd[h] for h in heads]

    p = {}
    for i in range(1, nblk):
        lo, hi = i * HG_SUB, (i + 1) * HG_SUB
        for h in heads:
            r = bs[h][lo - 1:lo]
            qd = qs[h][lo:hi] * jnp.exp(bs[h][lo:hi] - r)
            kp = ks[h][:lo] * jnp.exp(r - bs[h][:lo])
            p[i, h] = lax.dot_general(qd.astype(BF16), kp.astype(BF16), nt, preferred_element_type=F32)
    off = {}
    for i in range(1, nblk):
        lo = i * HG_SUB
        for h in heads:
            off[i, h] = jnp.dot(p[i, h].astype(BF16), vs[h][:lo].astype(BF16), preferred_element_type=F32)

    row = lax.broadcasted_iota(jnp.int32, (HG_SUB, 1), 0)
    outs = []
    for h in heads:
        blocks = []
        for i in range(nblk):
            lo, hi = i * HG_SUB, (i + 1) * HG_SUB
            bi, qi, ki, vi = bs[h][lo:hi], qs[h][lo:hi], ks[h][lo:hi], vs[h][lo:hi]
            oi = o_in[h][lo:hi]
            if i > 0:
                oi = oi + off[i, h]
            for s in range(HG_SUB):
                e = jnp.exp(bi - bi[s:s + 1])
                a = jnp.sum(qi * e * ki[s:s + 1], axis=-1, keepdims=True)
                a = jnp.where(row >= s, a, 0.0)
                oi = oi + a * vi[s:s + 1]
            blocks.append(oi)
        outs.append(jnp.concatenate(blocks, axis=0))
    return outs, st_new


def _hgrn_rec_kernel(q_ref, k_ref, v_ref, lf_ref, g_ref, nw_ref, o_ref, st_ref, *, n_chunks, dh):
    nh = st_ref.shape[0]

    @pl.when(pl.program_id(2) == 0)
    def _():
        st_ref[...] = jnp.zeros_like(st_ref)

    tri = _tri(HG_CHUNK)
    nw = nw_ref[...]

    def body(ci, carry):
        sl = pl.ds(pl.multiple_of(ci * HG_CHUNK, HG_CHUNK), HG_CHUNK)
        cols = [slice(h * dh, (h + 1) * dh) for h in range(nh)]
        outs, st_new = _hgrn_chunk([q_ref[0, sl, cs].astype(F32) for cs in cols],
                                   [k_ref[0, sl, cs].astype(F32) for cs in cols],
                                   [v_ref[0, sl, cs].astype(F32) for cs in cols], [lf_ref[0, sl, cs] for cs in cols],
                                   [st_ref[h] for h in range(nh)], tri)
        for h in range(nh):
            st_ref[h] = st_new[h]
            o = outs[h]
            o = o * lax.rsqrt(jnp.mean(o * o, axis=-1, keepdims=True) + RMS_EPS)
            o = o * nw * g_ref[0, sl, cols[h]].astype(F32)
            o_ref[0, sl, cols[h]] = o.astype(o_ref.dtype)
        return carry

    lax.fori_loop(0, n_chunks, body, 0)


def hgrn_recurrence(q, k, v, lf, g, norm_w, *, sblk=512, nhs=4):
    bsz, s, hd = q.shape
    dh = hd // HG_HEADS
    blk = pl.BlockSpec((1, sblk, nhs * dh), lambda b, h, t: (b, t, h))
    return pl.pallas_call(
        functools.partial(_hgrn_rec_kernel, n_chunks=sblk // HG_CHUNK, dh=dh),
        grid=(bsz, HG_HEADS // nhs, s // sblk),
        in_specs=[blk, blk, blk, blk, blk, pl.BlockSpec((1, dh), lambda b, h, t: (0, 0))],
        out_specs=blk,
        out_shape=jax.ShapeDtypeStruct((bsz, s, hd), BF16),
        scratch_shapes=[pltpu.VMEM((nhs, dh, dh), F32)],
        compiler_params=_params("parallel", "parallel", "arbitrary"),
        name="hgrn_recurrence",
    )(q, k, v, lf, g, norm_w)


def _mm_ln_kernel(h_ref, w_ref, x_ref, g_ref, b_ref, o_ref, *, alpha):
    y = jnp.dot(h_ref[...], w_ref[...], preferred_element_type=F32)
    o_ref[...] = _layer_norm(alpha * x_ref[...] + y, g_ref[...], b_ref[...])


def mm_ln(h, w, x2, g, b, *, alpha, tm=512):
    m, kdim = h.shape
    d = w.shape[1]
    row = lambda i: (i, 0)
    fixed = lambda i: (0, 0)
    return pl.pallas_call(
        functools.partial(_mm_ln_kernel, alpha=alpha),
        grid=(m // tm,),
        in_specs=[pl.BlockSpec((tm, kdim), row), pl.BlockSpec((kdim, d), fixed),
                  pl.BlockSpec((tm, d), row), pl.BlockSpec((1, d), fixed), pl.BlockSpec((1, d), fixed)],
        out_specs=pl.BlockSpec((tm, d), row),
        out_shape=jax.ShapeDtypeStruct((m, d), F32),
        compiler_params=_params("parallel"),
        name="mm_ln",
    )(h, w, x2, g, b)


def _ffn_ln_kernel(x_ref, wg_ref, wu_ref, wd_ref, g_ref, b_ref, o_ref, *, alpha):
    xb = x_ref[...].astype(BF16)
    a = jnp.dot(xb, wg_ref[...], preferred_element_type=F32)
    u = jnp.dot(xb, wu_ref[...], preferred_element_type=F32)
    y = jnp.dot((_silu(a) * u).astype(BF16), wd_ref[...], preferred_element_type=F32)
    o_ref[...] = _layer_norm(alpha * x_ref[...] + y, g_ref[...], b_ref[...])


def ffn_ln(x2, w_gu, w_down, g, b, *, alpha, tm=512):
    m, d = x2.shape
    f = w_down.shape[0]
    row = lambda i: (i, 0)
    fixed = lambda i: (0, 0)
    once = pl.Buffered(1)
    return pl.pallas_call(
        functools.partial(_ffn_ln_kernel, alpha=alpha),
        grid=(m // tm,),
        in_specs=[pl.BlockSpec((tm, d), row),
                  pl.BlockSpec((d, f), lambda i: (0, 0), pipeline_mode=once),
                  pl.BlockSpec((d, f), lambda i: (0, 1), pipeline_mode=once),
                  pl.BlockSpec((f, d), fixed, pipeline_mode=once),
                  pl.BlockSpec((1, d), fixed), pl.BlockSpec((1, d), fixed)],
        out_specs=pl.BlockSpec((tm, d), row),
        out_shape=jax.ShapeDtypeStruct((m, d), F32),
        compiler_params=_params("parallel"),
        name="ffn_ln",
    )(x2, w_gu, w_gu, w_down, g, b)


def _router_kernel(x_ref, wh_ref, wl_ref, o_ref):
    x = x_ref[...]
    xh = x.astype(BF16)
    xl = (x - xh.astype(F32)).astype(BF16)
    wh, wl = wh_ref[...], wl_ref[...]
    logits = (jnp.dot(xh, wh, preferred_element_type=F32)
              + (jnp.dot(xl, wh, preferred_element_type=F32) + jnp.dot(xh, wl, preferred_element_type=F32)))
    lane = lax.broadcasted_iota(jnp.int32, logits.shape, 1)
    logits = jnp.where(lane < N_EXPERTS, logits, -jnp.inf)
    m1 = jnp.max(logits, axis=-1, keepdims=True)
    i1 = jnp.min(jnp.where(logits == m1, lane, LANES), axis=-1, keepdims=True)
    rest = jnp.where(lane == i1, -jnp.inf, logits)
    m2 = jnp.max(rest, axis=-1, keepdims=True)
    i2 = jnp.min(jnp.where(rest == m2, lane, LANES), axis=-1, keepdims=True)
    e2 = jnp.exp(m2 - m1)
    den = 1.0 + e2
    w1 = 1.0 / den
    w2 = e2 / den
    o_ref[...] = jnp.where(lane == 0, i1.astype(F32),
                           jnp.where(lane == 1, i2.astype(F32),
                                     jnp.where(lane == 2, w1, jnp.where(lane == 3, w2, 0.0))))


def router(x2, w_hi, w_lo, *, tm=1024):
    m, d = x2.shape
    row = lambda i: (i, 0)
    fixed = lambda i: (0, 0)
    return pl.pallas_call(
        _router_kernel,
        grid=(m // tm,),
        in_specs=[pl.BlockSpec((tm, d), row), pl.BlockSpec((d, LANES), fixed), pl.BlockSpec((d, LANES), fixed)],
        out_specs=pl.BlockSpec((tm, LANES), row),
        out_shape=jax.ShapeDtypeStruct((m, LANES), F32),
        compiler_params=_params("parallel"),
        name="moe_router",
    )(x2, w_hi, w_lo)


def _row_copy(src_hbm, dst_vmem, sem, src_row, dst_row):
    return pltpu.make_async_copy(src_hbm.at[pl.ds(src_row, 1)], dst_vmem.at[pl.ds(dst_row, 1)], sem)


def _gather_start(idx_ref, base, src_hbm, dst_vmem, sem, n_rows):
    for r in range(n_rows):
        _row_copy(src_hbm, dst_vmem, sem, idx_ref[base + r], r).start(priority=r % 2)


def _gather_wait(src_hbm, dst_vmem, sem, n_rows):
    pltpu.make_async_copy(src_hbm.at[pl.ds(0, n_rows)], dst_vmem, sem).wait()


def _gather_kernel(idx_ref, src_hbm, o_ref, sem, *, rows):
    _gather_start(idx_ref, pl.program_id(0) * rows, src_hbm, o_ref, sem, rows)
    _gather_wait(src_hbm, o_ref, sem, rows)


def gather_rows(src, idx, *, rows=1024):
    n = idx.shape[0]
    d = src.shape[1]
    return pl.pallas_call(
        functools.partial(_gather_kernel, rows=rows),
        grid_spec=pltpu.PrefetchScalarGridSpec(
            num_scalar_prefetch=1,
            grid=(n // rows,),
            in_specs=[pl.BlockSpec(memory_space=pl.ANY)],
            out_specs=pl.BlockSpec((rows, d), lambda i, idx: (i, 0)),
            scratch_shapes=[pltpu.SemaphoreType.DMA(())]),
        out_shape=jax.ShapeDtypeStruct((n, d), src.dtype),
        compiler_params=_params("arbitrary"),
        name="moe_gather",
    )(idx, src)


def _experts_kernel(te_ref, nu_ref, xs_ref, wg_ref, wu_ref, wd_ref, o_ref, acc_ref):
    r, c = pl.program_id(0), pl.program_id(1)

    @pl.when(r < nu_ref[0])
    def _():
        xb = xs_ref[...].astype(BF16)
        a = jnp.dot(xb, wg_ref[0], preferred_element_type=F32)
        u = jnp.dot(xb, wu_ref[0], preferred_element_type=F32)
        part = jnp.dot((_silu(a) * u).astype(BF16), wd_ref[0], preferred_element_type=F32)

        @pl.when(c == 0)
        def _():
            acc_ref[...] = part

        @pl.when(c > 0)
        def _():
            acc_ref[...] += part

        @pl.when(c == pl.num_programs(1) - 1)
        def _():
            o_ref[...] = acc_ref[...]

    @pl.when(jnp.logical_and(r >= nu_ref[0], c == pl.num_programs(1) - 1))
    def _():
        o_ref[...] = jnp.zeros_like(o_ref)


def experts(xs, w_gu, w_down, tile_expert, n_used, *, tm, th=1792):
    n, d = xs.shape
    f = w_down.shape[1]
    nc = f // th
    return pl.pallas_call(
        _experts_kernel,
        grid_spec=pltpu.PrefetchScalarGridSpec(
            num_scalar_prefetch=2,
            grid=(n // tm, nc),
            in_specs=[pl.BlockSpec((tm, d), lambda r, c, te, nu: (r, 0)),
                      pl.BlockSpec((1, d, th), lambda r, c, te, nu: (te[r], 0, c)),
                      pl.BlockSpec((1, d, th), lambda r, c, te, nu: (te[r], 0, nc + c)),
                      pl.BlockSpec((1, th, d), lambda r, c, te, nu: (te[r], c, 0))],
            out_specs=pl.BlockSpec((tm, d), lambda r, c, te, nu: (r, 0)),
            scratch_shapes=[pltpu.VMEM((tm, d), F32)]),
        out_shape=jax.ShapeDtypeStruct((n, d), F32),
        compiler_params=_params("arbitrary", "arbitrary"),
        name="moe_experts",
    )(tile_expert, n_used, xs, w_gu, w_gu, w_down)


def _combine_ln_kernel(s1_ref, s2_ref, ys_hbm, x_ref, rw_ref, g_ref, b_ref, o_ref, a_buf, b_buf, sem, *,
                       alpha, rows):
    base = pl.program_id(0) * rows
    _gather_start(s1_ref, base, ys_hbm, a_buf, sem.at[0], rows)
    _gather_start(s2_ref, base, ys_hbm, b_buf, sem.at[1], rows)
    _gather_wait(ys_hbm, a_buf, sem.at[0], rows)
    _gather_wait(ys_hbm, b_buf, sem.at[1], rows)
    rw = rw_ref[...]
    y = rw[:, 2:3] * a_buf[...] + rw[:, 3:4] * b_buf[...]
    o_ref[...] = _layer_norm(alpha * x_ref[...] + y, g_ref[...], b_ref[...])


def combine_ln(ys, slot1, slot2, x2, rw, g, b, *, alpha, rows=512):
    m, d = x2.shape
    row = lambda i, s1, s2: (i, 0)
    fixed = lambda i, s1, s2: (0, 0)
    return pl.pallas_call(
        functools.partial(_combine_ln_kernel, alpha=alpha, rows=rows),
        grid_spec=pltpu.PrefetchScalarGridSpec(
            num_scalar_prefetch=2,
            grid=(m // rows,),
            in_specs=[pl.BlockSpec(memory_space=pl.ANY),
                      pl.BlockSpec((rows, d), row), pl.BlockSpec((rows, LANES), row),
                      pl.BlockSpec((1, d), fixed), pl.BlockSpec((1, d), fixed)],
            out_specs=pl.BlockSpec((rows, d), row),
            scratch_shapes=[pltpu.VMEM((rows, d), F32), pltpu.VMEM((rows, d), F32),
                            pltpu.SemaphoreType.DMA((2,))]),
        out_shape=jax.ShapeDtypeStruct((m, d), F32),
        compiler_params=_params("arbitrary"),
        name="moe_combine_ln",
    )(slot1, slot2, ys, x2, rw, g, b)


def moe_ln(x2, w_r_hi, w_r_lo, w_gu, w_down, g, b, *, alpha, tm=512):
    t = x2.shape[0]
    rw = router(x2, w_r_hi, w_r_lo)
    e_idx = rw[:, :TOP_K].astype(jnp.int32)
    flat_e = e_idx.reshape(-1)
    onehot = (flat_e[:, None] == jnp.arange(N_EXPERTS, dtype=jnp.int32)[None, :]).astype(jnp.int32)
    rank = jnp.take_along_axis(jnp.cumsum(onehot, axis=0) - onehot, flat_e[:, None], axis=1)[:, 0]
    counts = jnp.sum(onehot, axis=0)
    padded = ((counts + tm - 1) // tm) * tm
    ends = jnp.cumsum(padded)
    starts = ends - padded
    slot = starts[flat_e] + rank
    n_tiles = (TOP_K * t) // tm + N_EXPERTS
    n_slots = n_tiles * tm
    token = jnp.arange(TOP_K * t, dtype=jnp.int32) // TOP_K
    token_of_slot = jnp.zeros((n_slots,), jnp.int32).at[slot].set(token)
    tile_start = jnp.arange(n_tiles, dtype=jnp.int32) * tm
    tile_expert = jnp.minimum(jnp.sum((tile_start[:, None] >= ends[None, :]).astype(jnp.int32), axis=1),
                              N_EXPERTS - 1).astype(jnp.int32)
    n_used = (ends[-1:] // tm).astype(jnp.int32)

    xs = gather_rows(x2, token_of_slot)
    ys = experts(xs, w_gu, w_down, tile_expert, n_used, tm=tm)
    slot2d = slot.reshape(t, TOP_K).astype(jnp.int32)
    return combine_ln(ys, slot2d[:, 0], slot2d[:, 1], x2, rw, g, b, alpha=alpha)


def _mm_kernel(x_ref, w_ref, bias_ref, o_ref, *, kind, scale):
    acc = jnp.dot(x_ref[...].astype(BF16), w_ref[...], preferred_element_type=F32)
    if kind == "scale":
        acc = acc * scale
    elif kind == "sigmoid":
        acc = jax.nn.sigmoid(acc)
    elif kind == "log_sigmoid_bias":
        acc = _log_sigmoid(acc + bias_ref[...])
    o_ref[...] = acc.astype(o_ref.dtype)


def mm(x2, w, *, kind="none", scale=1.0, bias=None, out_dtype=BF16, tm=1024, tn=None):
    m, d = x2.shape
    n = w.shape[1]
    tn = n if tn is None else tn
    if bias is None:
        bias = jnp.zeros((1, n), F32)
    return pl.pallas_call(
        functools.partial(_mm_kernel, kind=kind, scale=scale),
        grid=(m // tm, n // tn),
        in_specs=[pl.BlockSpec((tm, d), lambda i, j: (i, 0)), pl.BlockSpec((d, tn), lambda i, j: (0, j)),
                  pl.BlockSpec((1, tn), lambda i, j: (0, j))],
        out_specs=pl.BlockSpec((tm, tn), lambda i, j: (i, j)),
        out_shape=jax.ShapeDtypeStruct((m, n), out_dtype),
        compiler_params=_params("parallel", "arbitrary"),
        name="mm_" + kind,
    )(x2, w, bias)


def _seq_cumsum_kernel(x_ref, o_ref, *, blk):
    tri = _tri(blk)
    n = x_ref.shape[1] // blk

    def body(i, carry):
        sl = pl.ds(pl.multiple_of(i * blk, blk), blk)
        c = _cumsum_rows(x_ref[0, sl, :], tri) + carry
        o_ref[0, sl, :] = c
        return c[blk - 1:blk]

    lax.fori_loop(0, n, body, jnp.zeros((1, x_ref.shape[2]), F32))


def seq_cumsum(x, *, blk=256):
    bsz, s, n = x.shape
    spec = pl.BlockSpec((1, s, n), lambda b: (b, 0, 0))
    return pl.pallas_call(
        functools.partial(_seq_cumsum_kernel, blk=blk),
        grid=(bsz,), in_specs=[spec], out_specs=spec,
        out_shape=jax.ShapeDtypeStruct(x.shape, F32),
        compiler_params=_params("parallel"),
        name="seq_cumsum",
    )(x)


def _split3_f32(c):
    return tuple(t.astype(F32) for t in _split3(c))


def _aug_cols(c3, col, width, first):
    hi, mid, lo = (t[:, col:col + 1] for t in c3)
    lane = lax.broadcasted_iota(jnp.int32, (hi.shape[0], width), 1)
    fill = jnp.where(lane < 6, 1.0 if first == 0 else -1.0, 0.0)
    return jnp.where(lane == first, hi, jnp.where(lane == first + 1, mid, jnp.where(lane == first + 2, lo, fill)))


def _fox_kernel(q_ref, cq_ref, ka_ref, vt_ref, gate_ref, o_ref, acc_ref, qt_ref, *, tq, tk):
    qi = pl.program_id(2)
    nhs, dh = acc_ref.shape[0], acc_ref.shape[1]
    rowi = lax.broadcasted_iota(jnp.int32, (tk, tq), 0)
    coli = lax.broadcasted_iota(jnp.int32, (tk, tq), 1)
    acc_ref[...] = jnp.zeros_like(acc_ref)
    c3 = _split3_f32(cq_ref[0, 0])
    for j in range(nhs):
        qa = jnp.concatenate([q_ref[0, :, j * dh:(j + 1) * dh].astype(F32), _aug_cols(c3, j, dh, 3)], axis=1)
        qt_ref[j] = qa.astype(BF16)

    def tile(ki, stats, masked):
        ks = pl.ds(pl.multiple_of(ki * tk, tk), tk)
        heads = range(len(stats))
        nt = (((1,), (1,)), ((), ()))
        ss = [lax.dot_general(ka_ref[0, j, ks, :], qt_ref[j], nt, preferred_element_type=F32) for j in heads]
        if masked:
            ss = [jnp.where(rowi <= coli, s, NEG_BIG) for s in ss]
        new, alphas, ps = [], [], []
        for j in heads:
            m_old, l_old = stats[j]
            m_new = jnp.maximum(m_old, jnp.max(ss[j], axis=0, keepdims=True))
            alpha = jnp.exp(m_old - m_new)
            p = jnp.exp(ss[j] - m_new)
            new.append((m_new, alpha * l_old + jnp.sum(p, axis=0, keepdims=True)))
            alphas.append(alpha)
            ps.append(p.astype(BF16))
        pvs = [jnp.dot(vt_ref[0, j, :, ks], ps[j], preferred_element_type=F32) for j in heads]
        for j in heads:
            acc_ref[j] = alphas[j] * acc_ref[j] + pvs[j]
        return tuple(new)

    init = tuple((jnp.full((1, tq), NEG_BIG, F32), jnp.zeros((1, tq), F32)) for _ in range(nhs))
    stats = lax.fori_loop(0, qi, lambda ki, st: tile(ki, st, False), init)
    stats = tile(qi, stats, True)
    ot = jnp.concatenate([acc_ref[j] / stats[j][1] for j in range(nhs)], axis=0)
    o_ref[0] = (ot.T * gate_ref[0].astype(F32)).astype(o_ref.dtype)


def fox_attention(q, cq, ka, vt, gate, *, tq=256, nhs=8):
    bsz, nh, dh, s = vt.shape
    aug = ka.shape[3]
    hp = nh // nhs
    row = pl.BlockSpec((1, tq, nhs * dh), lambda b, h, t: (b, t, h))
    return pl.pallas_call(
        functools.partial(_fox_kernel, tq=tq, tk=tq),
        grid=(bsz, hp, s // tq),
        in_specs=[row,
                  pl.BlockSpec((1, 1, tq, nhs), lambda b, h, t: (b, h, t, 0)),
                  pl.BlockSpec((1, nhs, s, aug), lambda b, h, t: (b, h, 0, 0)),
                  pl.BlockSpec((1, nhs, dh, s), lambda b, h, t: (b, h, 0, 0)),
                  row],
        out_specs=row,
        out_shape=jax.ShapeDtypeStruct((bsz, s, nh * dh), BF16),
        scratch_shapes=[pltpu.VMEM((nhs, dh, tq), F32), pltpu.VMEM((nhs, tq, aug), BF16)],
        compiler_params=_params("parallel", "parallel", "arbitrary"),
        name="fox_attention",
    )(q, cq, ka, vt, gate)


def _fox_prep_kernel(k_ref, v_ref, c_ref, ka_ref, vt_ref):
    nh, dh = vt_ref.shape[1], vt_ref.shape[2]
    c3 = _split3_f32(c_ref[0])
    for h in range(nh):
        aug = _aug_cols(c3, h, ka_ref.shape[3] - dh, 0)
        ka_ref[0, h] = jnp.concatenate([k_ref[0, :, h * dh:(h + 1) * dh].astype(F32), aug], axis=1).astype(BF16)
    per = LANES // dh
    for g in range(nh // per):
        vt2 = v_ref[0, :, g * LANES:(g + 1) * LANES].astype(F32).T
        for j in range(per):
            vt_ref[0, g * per + j] = vt2[j * dh:(j + 1) * dh].astype(BF16)


def fox_prep_kv(k_sh, v_sh, c, *, ts=256):
    bsz, s, hd = k_sh.shape
    nh, dh = FOX_HEADS, FOX_HEAD_DIM
    row = pl.BlockSpec((1, ts, hd), lambda b, t: (b, t, 0))
    return pl.pallas_call(
        _fox_prep_kernel,
        grid=(bsz, s // ts),
        in_specs=[row, row, pl.BlockSpec((1, ts, LANES), lambda b, t: (b, t, 0))],
        out_specs=[pl.BlockSpec((1, nh, ts, LANES), lambda b, t: (b, 0, t, 0)),
                   pl.BlockSpec((1, nh, dh, ts), lambda b, t: (b, 0, 0, t))],
        out_shape=[jax.ShapeDtypeStruct((bsz, nh, s, LANES), BF16),
                   jax.ShapeDtypeStruct((bsz, nh, dh, s), BF16)],
        compiler_params=_params("parallel", "parallel"),
        name="fox_prep_kv",
    )(k_sh, v_sh, c)


def kernel(x, hg_w_in, hg_lb_logits, hg_norm_w, hg_w_out, kv_w, kv_fgate_b, fox_w_qg, fox_w_out, ln_mix_g, ln_mix_b, ln_ffn_g, ln_ffn_b, ffn_w_gu, ffn_w_down, moe_w_router, moe_w_gu, moe_w_down):
    bsz, s, d = x.shape
    depth = ln_mix_g.shape[0]
    n_a = hg_w_in.shape[0]
    alpha = (2 * depth) ** 0.25
    fox_dim = FOX_HEADS * FOX_HEAD_DIM
    t = bsz * s

    p = jax.nn.softmax(hg_lb_logits.astype(F32), axis=0)
    lbs = jnp.cumsum(p, axis=0) - p[0]

    def vec(a):
        return a.reshape(1, -1).astype(F32)

    x2 = x.reshape(t, d)
    ka = vt = cq = None
    for layer in range(depth):
        if layer < n_a:
            q, kk, v, lf, g = hgrn_proj(x2, hg_w_in[layer].astype(BF16), vec(lbs[layer]))
            sh = (bsz, s, -1)
            o = hgrn_recurrence(q.reshape(sh), kk.reshape(sh), v.reshape(sh), lf.reshape(sh), g.reshape(sh),
                                vec(hg_norm_w[layer]))
            x2 = mm_ln(o.reshape(t, -1), hg_w_out[layer].astype(BF16), x2, vec(ln_mix_g[layer]),
                       vec(ln_mix_b[layer]), alpha=alpha)
        else:
            bi = layer - n_a
            w_qg = fox_w_qg[bi].astype(BF16)
            q = mm(x2, w_qg[:, :fox_dim], kind="scale", scale=FOX_HEAD_DIM ** -0.5)
            gate = mm(x2, w_qg[:, fox_dim:], kind="sigmoid")
            sh = (bsz, s, fox_dim)
            o = fox_attention(q.reshape(sh), cq, ka, vt, gate.reshape(sh), nhs=FOX_HEADS_PER_STEP)
            x2 = mm_ln(o.reshape(t, fox_dim), fox_w_out[bi].astype(BF16), x2, vec(ln_mix_g[layer]),
                       vec(ln_mix_b[layer]), alpha=alpha)
        if layer % 2 == 0:
            x2 = ffn_ln(x2, ffn_w_gu[layer // 2].astype(BF16), ffn_w_down[layer // 2].astype(BF16),
                        vec(ln_ffn_g[layer]), vec(ln_ffn_b[layer]), alpha=alpha)
        else:
            w_r = jnp.pad(moe_w_router[layer // 2].astype(F32), ((0, 0), (0, LANES - N_EXPERTS)))
            w_r_hi = w_r.astype(BF16)
            w_r_lo = (w_r - w_r_hi.astype(F32)).astype(BF16)
            x2 = moe_ln(x2, w_r_hi, w_r_lo, moe_w_gu[layer // 2].astype(BF16), moe_w_down[layer // 2].astype(BF16),
                        vec(ln_ffn_g[layer]), vec(ln_ffn_b[layer]), alpha=alpha)
        if layer == n_a - 1:
            w_kv = kv_w.astype(BF16)
            k_sh = mm(x2, w_kv[:, :fox_dim]).reshape(bsz, s, fox_dim)
            v_sh = mm(x2, w_kv[:, fox_dim:2 * fox_dim]).reshape(bsz, s, fox_dim)
            w_f = jnp.pad(w_kv[:, 2 * fox_dim:], ((0, 0), (0, LANES - FOX_HEADS)))
            b_f = jnp.pad(kv_fgate_b.astype(F32), (0, LANES - FOX_HEADS)).reshape(1, LANES)
            log_f = mm(x2, w_f, kind="log_sigmoid_bias", bias=b_f, out_dtype=F32)
            c = seq_cumsum(log_f.reshape(bsz, s, LANES))
            ka, vt = fox_prep_kv(k_sh, v_sh, c)
            cq = c[:, :, :FOX_HEADS].reshape(bsz, s, FOX_HEADS // FOX_HEADS_PER_STEP, FOX_HEADS_PER_STEP)
            cq = cq.transpose(0, 2, 1, 3)
    return x2.reshape(bsz, s, d)
```

```python
import functools

import jax
import jax.numpy as jnp
from jax import lax
from jax.experimental import pallas as pl
from jax.experimental.pallas import tpu as pltpu

F32 = jnp.float32
BF16 = jnp.bfloat16

HG_HEADS = 8
HG_CHUNK = 64
HG_SUB = 8
LB_FLOOR = 1e-30
FOX_HEADS = 16
FOX_HEAD_DIM = 64
FOX_HEADS_PER_STEP = 8
N_EXPERTS = 8
TOP_K = 2
LN_EPS = 1e-5
RMS_EPS = 1e-6

LANES = 128
VMEM_LIMIT_BYTES = 56 * 1024 * 1024

NEG_BIG = -1e30


def _params(*sem):
    return pltpu.CompilerParams(dimension_semantics=sem, vmem_limit_bytes=VMEM_LIMIT_BYTES)


def _layer_norm(h, g, b):
    mu = jnp.mean(h, axis=-1, keepdims=True)
    d = h - mu
    var = jnp.mean(d * d, axis=-1, keepdims=True)
    return d * lax.rsqrt(var + LN_EPS) * g + b


def _silu(x):
    return x * jax.nn.sigmoid(x)


def _log1p_exp_neg(d):
    return jnp.log(1.0 + jnp.exp(-d))


def _log_sigmoid(z):
    return jnp.minimum(z, 0.0) - _log1p_exp_neg(jnp.abs(z))


def _split3(x):
    hi = x.astype(BF16)
    r1 = x - hi.astype(F32)
    mid = r1.astype(BF16)
    lo = (r1 - mid.astype(F32)).astype(BF16)
    return hi, mid, lo


def _tri(n):
    r = lax.broadcasted_iota(jnp.int32, (n, n), 0)
    c = lax.broadcasted_iota(jnp.int32, (n, n), 1)
    return (r >= c).astype(BF16)


def _cumsum_rows(x, tri):
    hi, mid, lo = _split3(x)
    return (jnp.dot(tri, hi, preferred_element_type=F32)
            + jnp.dot(tri, mid, preferred_element_type=F32)
            + jnp.dot(tri, lo, preferred_element_type=F32))


def _hgrn_gate_kernel(x_ref, w_ref, lb_ref, lf_ref, k_ref):
    z = jnp.dot(x_ref[...].astype(BF16), w_ref[...], preferred_element_type=F32)
    lb = lb_ref[...]
    log_lb = jnp.log(jnp.maximum(lb, LB_FLOOR))
    a = _log_sigmoid(z)
    b = log_lb + _log_sigmoid(-z)
    lf_ref[...] = jnp.maximum(a, b) + _log1p_exp_neg(jnp.abs(a - b))
    k_ref[...] = ((1.0 - lb) * jax.nn.sigmoid(-z)).astype(k_ref.dtype)


def hgrn_gate(x2, w_f, lb, *, tm=512):
    m, d = x2.shape
    n = w_f.shape[1]
    row = lambda i: (i, 0)
    fixed = lambda i: (0, 0)
    return pl.pallas_call(
        _hgrn_gate_kernel,
        grid=(m // tm,),
        in_specs=[pl.BlockSpec((tm, d), row), pl.BlockSpec((d, n), fixed), pl.BlockSpec((1, n), fixed)],
        out_specs=[pl.BlockSpec((tm, n), row)] * 2,
        out_shape=[jax.ShapeDtypeStruct((m, n), F32), jax.ShapeDtypeStruct((m, n), BF16)],
        compiler_params=_params("parallel"),
        name="hgrn_gate",
    )(x2, w_f, lb)


def _hgrn_chunk(qs, ks, vs, lfs, sts, tri):
    heads = range(len(qs))
    c = qs[0].shape[0]
    nblk = c // HG_SUB
    nt = (((1,), (1,)), ((), ()))
    tn = (((0,), (0,)), ((), ()))
    bs = [_cumsum_rows(lfs[h], tri) for h in heads]
    o_in = [lax.dot_general((qs[h] * jnp.exp(bs[h])).astype(BF16), sts[h].astype(BF16), nt,
                            preferred_element_type=F32) for h in heads]
    b_last = [bs[h][c - 1:c] for h in heads]
    upd = [lax.dot_general(vs[h].astype(BF16), (ks[h] * jnp.exp(b_last[h] - bs[h])).astype(BF16), tn,
                           preferred_element_type=F32) for h in heads]
    st_new = [sts[h] * jnp.exp(b_last[h]) + upd[h] for h in heads]

    p = {}
    for i in range(1, nblk):
        lo, hi = i * HG_SUB, (i + 1) * HG_SUB
        for h in heads:
            r = bs[h][lo - 1:lo]
            qd = qs[h][lo:hi] * jnp.exp(bs[h][lo:hi] - r)
            kp = ks[h][:lo] * jnp.exp(r - bs[h][:lo])
            p[i, h] = lax.dot_general(qd.astype(BF16), kp.astype(BF16), nt, preferred_element_type=F32)
    off = {}
    for i in range(1, nblk):
        lo = i * HG_SUB
        for h in heads:
            off[i, h] = jnp.dot(p[i, h].astype(BF16), vs[h][:lo].astype(BF16), preferred_element_type=F32)

    row = lax.broadcasted_iota(jnp.int32, (HG_SUB, 1), 0)
    outs = []
    for h in heads:
        blocks = []
        for i in range(nblk):
            lo, hi = i * HG_SUB, (i + 1) * HG_SUB
            bi, qi, ki, vi = bs[h][lo:hi], qs[h][lo:hi], ks[h][lo:hi], vs[h][lo:hi]
            oi = o_in[h][lo:hi]
            if i > 0:
                oi = oi + off[i, h]
            for s in range(HG_SUB):
                e = jnp.exp(bi - bi[s:s + 1])
                a = jnp.sum(qi * e * ki[s:s + 1], axis=-1, keepdims=True)
                a = jnp.where(row >= s, a, 0.0)
                oi = oi + a * vi[s:s + 1]
            blocks.append(oi)
        outs.append(jnp.concatenate(blocks, axis=0))
    return outs, st_new


def _hgrn_rec_kernel(q_ref, k_ref, v_ref, lf_ref, g_ref, nw_ref, o_ref, st_ref, *, n_chunks, dh):
    nh = st_ref.shape[0]

    @pl.when(pl.program_id(2) == 0)
    def _():
        st_ref[...] = jnp.zeros_like(st_ref)

    tri = _tri(HG_CHUNK)
    nw = nw_ref[...]

    def body(ci, carry):
        sl = pl.ds(pl.multiple_of(ci * HG_CHUNK, HG_CHUNK), HG_CHUNK)
        cols = [slice(h * dh, (h + 1) * dh) for h in range(nh)]
        outs, st_new = _hgrn_chunk([q_ref[0, sl, cs].astype(F32) for cs in cols],
                                   [k_ref[0, sl, cs].astype(F32) for cs in cols],
                                   [v_ref[0, sl, cs].astype(F32) for cs in cols], [lf_ref[0, sl, cs] for cs in cols],
                                   [st_ref[h] for h in range(nh)], tri)
        for h in range(nh):
            st_ref[h] = st_new[h]
            o = outs[h]
            o = o * lax.rsqrt(jnp.mean(o * o, axis=-1, keepdims=True) + RMS_EPS)
            o = o * nw * g_ref[0, sl, cols[h]].astype(F32)
            o_ref[0, sl, cols[h]] = o.astype(o_ref.dtype)
        return carry

    lax.fori_loop(0, n_chunks, body, 0)


def hgrn_recurrence(q, k, v, lf, g, norm_w, *, sblk=512, nhs=4):
    bsz, s, hd = q.shape
    dh = hd // HG_HEADS
    blk = pl.BlockSpec((1, sblk, nhs * dh), lambda b, h, t: (b, t, h))
    return pl.pallas_call(
        functools.partial(_hgrn_rec_kernel, n_chunks=sblk // HG_CHUNK, dh=dh),
        grid=(bsz, HG_HEADS // nhs, s // sblk),
        in_specs=[blk, blk, blk, blk, blk, pl.BlockSpec((1, dh), lambda b, h, t: (0, 0))],
        out_specs=blk,
        out_shape=jax.ShapeDtypeStruct((bsz, s, hd), BF16),
        scratch_shapes=[pltpu.VMEM((nhs, dh, dh), F32)],
        compiler_params=_params("parallel", "parallel", "arbitrary"),
        name="hgrn_recurrence",
    )(q, k, v, lf, g, norm_w)


def _mm_ln_kernel(h_ref, w_ref, x_ref, g_ref, b_ref, o_ref, *, alpha):
    y = jnp.dot(h_ref[...], w_ref[...], preferred_element_type=F32)
    o_ref[...] = _layer_norm(alpha * x_ref[...] + y, g_ref[...], b_ref[...])


def mm_ln(h, w, x2, g, b, *, alpha, tm=512):
    m, kdim = h.shape
    d = w.shape[1]
    row = lambda i: (i, 0)
    fixed = lambda i: (0, 0)
    return pl.pallas_call(
        functools.partial(_mm_ln_kernel, alpha=alpha),
        grid=(m // tm,),
        in_specs=[pl.BlockSpec((tm, kdim), row), pl.BlockSpec((kdim, d), fixed),
                  pl.BlockSpec((tm, d), row), pl.BlockSpec((1, d), fixed), pl.BlockSpec((1, d), fixed)],
        out_specs=pl.BlockSpec((tm, d), row),
        out_shape=jax.ShapeDtypeStruct((m, d), F32),
        compiler_params=_params("parallel"),
        name="mm_ln",
    )(h, w, x2, g, b)


def _ffn_ln_kernel(x_ref, wg_ref, wu_ref, wd_ref, g_ref, b_ref, o_ref, *, alpha):
    xb = x_ref[...].astype(BF16)
    a = jnp.dot(xb, wg_ref[...], preferred_element_type=F32)
    u = jnp.dot(xb, wu_ref[...], preferred_element_type=F32)
    y = jnp.dot((_silu(a) * u).astype(BF16), wd_ref[...], preferred_element_type=F32)
    o_ref[...] = _layer_norm(alpha * x_ref[...] + y, g_ref[...], b_ref[...])


def ffn_ln(x2, w_gu, w_down, g, b, *, alpha, tm=512):
    m, d = x2.shape
    f = w_down.shape[0]
    row = lambda i: (i, 0)
    fixed = lambda i: (0, 0)
    once = pl.Buffered(1)
    return pl.pallas_call(
        functools.partial(_ffn_ln_kernel, alpha=alpha),
        grid=(m // tm,),
        in_specs=[pl.BlockSpec((tm, d), row),
                  pl.BlockSpec((d, f), lambda i: (0, 0), pipeline_mode=once),
                  pl.BlockSpec((d, f), lambda i: (0, 1), pipeline_mode=once),
                  pl.BlockSpec((f, d), fixed, pipeline_mode=once),
                  pl.BlockSpec((1, d), fixed), pl.BlockSpec((1, d), fixed)],
        out_specs=pl.BlockSpec((tm, d), row),
        out_shape=jax.ShapeDtypeStruct((m, d), F32),
        compiler_params=_params("parallel"),
        name="ffn_ln",
    )(x2, w_gu, w_gu, w_down, g, b)


def _router_kernel(x_ref, wh_ref, wl_ref, o_ref):
    x = x_ref[...]
    xh = x.astype(BF16)
    xl = (x - xh.astype(F32)).astype(BF16)
    wh, wl = wh_ref[...], wl_ref[...]
    logits = (jnp.dot(xh, wh, preferred_element_type=F32)
              + (jnp.dot(xl, wh, preferred_element_type=F32) + jnp.dot(xh, wl, preferred_element_type=F32)))
    lane = lax.broadcasted_iota(jnp.int32, logits.shape, 1)
    logits = jnp.where(lane < N_EXPERTS, logits, -jnp.inf)
    m1 = jnp.max(logits, axis=-1, keepdims=True)
    i1 = jnp.min(jnp.where(logits == m1, lane, LANES), axis=-1, keepdims=True)
    rest = jnp.where(lane == i1, -jnp.inf, logits)
    m2 = jnp.max(rest, axis=-1, keepdims=True)
    i2 = jnp.min(jnp.where(rest == m2, lane, LANES), axis=-1, keepdims=True)
    e2 = jnp.exp(m2 - m1)
    den = 1.0 + e2
    w1 = 1.0 / den
    w2 = e2 / den
    o_ref[...] = jnp.where(lane == 0, i1.astype(F32),
                           jnp.where(lane == 1, i2.astype(F32),
                                     jnp.where(lane == 2, w1, jnp.where(lane == 3, w2, 0.0))))


def router(x2, w_hi, w_lo, *, tm=1024):
    m, d = x2.shape
    row = lambda i: (i, 0)
    fixed = lambda i: (0, 0)
    return pl.pallas_call(
        _router_kernel,
        grid=(m // tm,),
        in_specs=[pl.BlockSpec((tm, d), row), pl.BlockSpec((d, LANES), fixed), pl.BlockSpec((d, LANES), fixed)],
        out_specs=pl.BlockSpec((tm, LANES), row),
        out_shape=jax.ShapeDtypeStruct((m, LANES), F32),
        compiler_params=_params("parallel"),
        name="moe_router",
    )(x2, w_hi, w_lo)


def _row_copy(src_hbm, dst_vmem, sem, src_row, dst_row):
    return pltpu.make_async_copy(src_hbm.at[pl.ds(src_row, 1)], dst_vmem.at[pl.ds(dst_row, 1)], sem)


def _gather_start(idx_ref, base, src_hbm, dst_vmem, sem, n_rows):
    for r in range(n_rows):
        _row_copy(src_hbm, dst_vmem, sem, idx_ref[base + r], r).start(priority=r % 2)


def _gather_wait(src_hbm, dst_vmem, sem, n_rows):
    pltpu.make_async_copy(src_hbm.at[pl.ds(0, n_rows)], dst_vmem, sem).wait()


def _gather_kernel(idx_ref, src_hbm, o_ref, sem, *, rows):
    _gather_start(idx_ref, pl.program_id(0) * rows, src_hbm, o_ref, sem, rows)
    _gather_wait(src_hbm, o_ref, sem, rows)


def gather_rows(src, idx, *, rows=1024):
    n = idx.shape[0]
    d = src.shape[1]
    return pl.pallas_call(
        functools.partial(_gather_kernel, rows=rows),
        grid_spec=pltpu.PrefetchScalarGridSpec(
            num_scalar_prefetch=1,
            grid=(n // rows,),
            in_specs=[pl.BlockSpec(memory_space=pl.ANY)],
            out_specs=pl.BlockSpec((rows, d), lambda i, idx: (i, 0)),
            scratch_shapes=[pltpu.SemaphoreType.DMA(())]),
        out_shape=jax.ShapeDtypeStruct((n, d), src.dtype),
        compiler_params=_params("arbitrary"),
        name="moe_gather",
    )(idx, src)


def _experts_kernel(te_ref, nu_ref, xs_ref, wg_ref, wu_ref, wd_ref, o_ref, acc_ref):
    r, c = pl.program_id(0), pl.program_id(1)

    @pl.when(r < nu_ref[0])
    def _():
        xb = xs_ref[...].astype(BF16)
        a = jnp.dot(xb, wg_ref[0], preferred_element_type=F32)
        u = jnp.dot(xb, wu_ref[0], preferred_element_type=F32)
        part = jnp.dot((_silu(a) * u).astype(BF16), wd_ref[0], preferred_element_type=F32)

        @pl.when(c == 0)
        def _():
            acc_ref[...] = part

        @pl.when(c > 0)
        def _():
            acc_ref[...] += part

        @pl.when(c == pl.num_programs(1) - 1)
        def _():
            o_ref[...] = acc_ref[...]

    @pl.when(jnp.logical_and(r >= nu_ref[0], c == pl.num_programs(1) - 1))
    def _():
        o_ref[...] = jnp.zeros_like(o_ref)


def experts(xs, w_gu, w_down, tile_expert, n_used, *, tm, th=1792):
    n, d = xs.shape
    f = w_down.shape[1]
    nc = f // th
    return pl.pallas_call(
        _experts_kernel,
        grid_spec=pltpu.PrefetchScalarGridSpec(
            num_scalar_prefetch=2,
            grid=(n // tm, nc),
            in_specs=[pl.BlockSpec((tm, d), lambda r, c, te, nu: (r, 0)),
                      pl.BlockSpec((1, d, th), lambda r, c, te, nu: (te[r], 0, c)),
                      pl.BlockSpec((1, d, th), lambda r, c, te, nu: (te[r], 0, nc + c)),
                      pl.BlockSpec((1, th, d), lambda r, c, te, nu: (te[r], c, 0))],
            out_specs=pl.BlockSpec((tm, d), lambda r, c, te, nu: (r, 0)),
            scratch_shapes=[pltpu.VMEM((tm, d), F32)]),
        out_shape=jax.ShapeDtypeStruct((n, d), F32),
        compiler_params=_params("arbitrary", "arbitrary"),
        name="moe_experts",
    )(tile_expert, n_used, xs, w_gu, w_gu, w_down)


def _combine_ln_kernel(s1_ref, s2_ref, ys_hbm, x_ref, rw_ref, g_ref, b_ref, o_ref, a_buf, b_buf, sem, *,
                       alpha, rows):
    base = pl.program_id(0) * rows
    _gather_start(s1_ref, base, ys_hbm, a_buf, sem.at[0], rows)
    _gather_start(s2_ref, base, ys_hbm, b_buf, sem.at[1], rows)
    _gather_wait(ys_hbm, a_buf, sem.at[0], rows)
    _gather_wait(ys_hbm, b_buf, sem.at[1], rows)
    rw = rw_ref[...]
    y = rw[:, 2:3] * a_buf[...] + rw[:, 3:4] * b_buf[...]
    o_ref[...] = _layer_norm(alpha * x_ref[...] + y, g_ref[...], b_ref[...])


def combine_ln(ys, slot1, slot2, x2, rw, g, b, *, alpha, rows=512):
    m, d = x2.shape
    row = lambda i, s1, s2: (i, 0)
    fixed = lambda i, s1, s2: (0, 0)
    return pl.pallas_call(
        functools.partial(_combine_ln_kernel, alpha=alpha, rows=rows),
        grid_spec=pltpu.PrefetchScalarGridSpec(
            num_scalar_prefetch=2,
            grid=(m // rows,),
            in_specs=[pl.BlockSpec(memory_space=pl.ANY),
                      pl.BlockSpec((rows, d), row), pl.BlockSpec((rows, LANES), row),
                      pl.BlockSpec((1, d), fixed), pl.BlockSpec((1, d), fixed)],
            out_specs=pl.BlockSpec((rows, d), row),
            scratch_shapes=[pltpu.VMEM((rows, d), F32), pltpu.VMEM((rows, d), F32),
                            pltpu.SemaphoreType.DMA((2,))]),
        out_shape=jax.ShapeDtypeStruct((m, d), F32),
        compiler_params=_params("arbitrary"),
        name="moe_combine_ln",
    )(slot1, slot2, ys, x2, rw, g, b)


def moe_ln(x2, w_r_hi, w_r_lo, w_gu, w_down, g, b, *, alpha, tm=512):
    t = x2.shape[0]
    rw = router(x2, w_r_hi, w_r_lo)
    e_idx = rw[:, :TOP_K].astype(jnp.int32)
    flat_e = e_idx.reshape(-1)
    onehot = (flat_e[:, None] == jnp.arange(N_EXPERTS, dtype=jnp.int32)[None, :]).astype(jnp.int32)
    rank = jnp.take_along_axis(jnp.cumsum(onehot, axis=0) - onehot, flat_e[:, None], axis=1)[:, 0]
    counts = jnp.sum(onehot, axis=0)
    padded = ((counts + tm - 1) // tm) * tm
    ends = jnp.cumsum(padded)
    starts = ends - padded
    slot = starts[flat_e] + rank
    n_tiles = (TOP_K * t) // tm + N_EXPERTS
    n_slots = n_tiles * tm
    token = jnp.arange(TOP_K * t, dtype=jnp.int32) // TOP_K
    token_of_slot = jnp.zeros((n_slots,), jnp.int32).at[slot].set(token)
    tile_start = jnp.arange(n_tiles, dtype=jnp.int32) * tm
    tile_expert = jnp.minimum(jnp.sum((tile_start[:, None] >= ends[None, :]).astype(jnp.int32), axis=1),
                              N_EXPERTS - 1).astype(jnp.int32)
    n_used = (ends[-1:] // tm).astype(jnp.int32)

    xs = gather_rows(x2, token_of_slot)
    ys = experts(xs, w_gu, w_down, tile_expert, n_used, tm=tm)
    slot2d = slot.reshape(t, TOP_K).astype(jnp.int32)
    return combine_ln(ys, slot2d[:, 0], slot2d[:, 1], x2, rw, g, b, alpha=alpha)


def _mm_kernel(x_ref, w_ref, bias_ref, o_ref, *, kind, scale):
    acc = jnp.dot(x_ref[...].astype(BF16), w_ref[...], preferred_element_type=F32)
    if kind == "scale":
        acc = acc * scale
    elif kind == "sigmoid":
        acc = jax.nn.sigmoid(acc)
    elif kind == "silu":
        acc = _silu(acc)
    elif kind == "log_sigmoid_bias":
        acc = _log_sigmoid(acc + bias_ref[...])
    o_ref[...] = acc.astype(o_ref.dtype)


def mm(x2, w, *, kind="none", scale=1.0, bias=None, out_dtype=BF16, tm=1024, tn=None):
    m, d = x2.shape
    n = w.shape[1]
    tn = n if tn is None else tn
    if bias is None:
        bias = jnp.zeros((1, n), F32)
    return pl.pallas_call(
        functools.partial(_mm_kernel, kind=kind, scale=scale),
        grid=(m // tm, n // tn),
        in_specs=[pl.BlockSpec((tm, d), lambda i, j: (i, 0)), pl.BlockSpec((d, tn), lambda i, j: (0, j)),
                  pl.BlockSpec((1, tn), lambda i, j: (0, j))],
        out_specs=pl.BlockSpec((tm, tn), lambda i, j: (i, j)),
        out_shape=jax.ShapeDtypeStruct((m, n), out_dtype),
        compiler_params=_params("parallel", "arbitrary"),
        name="mm_" + kind,
    )(x2, w, bias)


def _seq_cumsum_kernel(x_ref, o_ref, *, blk):
    tri = _tri(blk)
    n = x_ref.shape[1] // blk

    def body(i, carry):
        sl = pl.ds(pl.multiple_of(i * blk, blk), blk)
        c = _cumsum_rows(x_ref[0, sl, :], tri) + carry
        o_ref[0, sl, :] = c
        return c[blk - 1:blk]

    lax.fori_loop(0, n, body, jnp.zeros((1, x_ref.shape[2]), F32))


def seq_cumsum(x, *, blk=256):
    bsz, s, n = x.shape
    spec = pl.BlockSpec((1, s, n), lambda b: (b, 0, 0))
    return pl.pallas_call(
        functools.partial(_seq_cumsum_kernel, blk=blk),
        grid=(bsz,), in_specs=[spec], out_specs=spec,
        out_shape=jax.ShapeDtypeStruct(x.shape, F32),
        compiler_params=_params("parallel"),
        name="seq_cumsum",
    )(x)


def _split3_f32(c):
    return tuple(t.astype(F32) for t in _split3(c))


def _aug_cols(c3, col, width, first):
    hi, mid, lo = (t[:, col:col + 1] for t in c3)
    lane = lax.broadcasted_iota(jnp.int32, (hi.shape[0], width), 1)
    fill = jnp.where(lane < 6, 1.0 if first == 0 else -1.0, 0.0)
    return jnp.where(lane == first, hi, jnp.where(lane == first + 1, mid, jnp.where(lane == first + 2, lo, fill)))


def _fox_kernel(q_ref, cq_ref, ka_ref, vt_ref, gate_ref, o_ref, acc_ref, qt_ref, *, tq, tk):
    qi = pl.program_id(2)
    nhs, dh = acc_ref.shape[0], acc_ref.shape[1]
    rowi = lax.broadcasted_iota(jnp.int32, (tk, tq), 0)
    coli = lax.broadcasted_iota(jnp.int32, (tk, tq), 1)
    acc_ref[...] = jnp.zeros_like(acc_ref)
    c3 = _split3_f32(cq_ref[0, 0])
    for j in range(nhs):
        qa = jnp.concatenate([q_ref[0, :, j * dh:(j + 1) * dh].astype(F32), _aug_cols(c3, j, dh, 3)], axis=1)
        qt_ref[j] = qa.astype(BF16)

    def tile(ki, stats, masked):
        ks = pl.ds(pl.multiple_of(ki * tk, tk), tk)
        heads = range(len(stats))
        nt = (((1,), (1,)), ((), ()))
        ss = [lax.dot_general(ka_ref[0, j, ks, :], qt_ref[j], nt, preferred_element_type=F32) for j in heads]
        if masked:
            ss = [jnp.where(rowi <= coli, s, NEG_BIG) for s in ss]
        new, alphas, ps = [], [], []
        for j in heads:
            m_old, l_old = stats[j]
            m_new = jnp.maximum(m_old, jnp.max(ss[j], axis=0, keepdims=True))
            alpha = jnp.exp(m_old - m_new)
            p = jnp.exp(ss[j] - m_new)
            new.append((m_new, alpha * l_old + jnp.sum(p, axis=0, keepdims=True)))
            alphas.append(alpha)
            ps.append(p.astype(BF16))
        pvs = [jnp.dot(vt_ref[0, j, :, ks], ps[j], preferred_element_type=F32) for j in heads]
        for j in heads:
            acc_ref[j] = alphas[j] * acc_ref[j] + pvs[j]
        return tuple(new)

    init = tuple((jnp.full((1, tq), NEG_BIG, F32), jnp.zeros((1, tq), F32)) for _ in range(nhs))
    stats = lax.fori_loop(0, qi, lambda ki, st: tile(ki, st, False), init)
    stats = tile(qi, stats, True)
    ot = jnp.concatenate([acc_ref[j] / stats[j][1] for j in range(nhs)], axis=0)
    o_ref[0] = (ot.T * gate_ref[0].astype(F32)).astype(o_ref.dtype)


def fox_attention(q, cq, ka, vt, gate, *, tq=256, nhs=8):
    bsz, nh, dh, s = vt.shape
    aug = ka.shape[3]
    hp = nh // nhs
    row = pl.BlockSpec((1, tq, nhs * dh), lambda b, h, t: (b, t, h))
    return pl.pallas_call(
        functools.partial(_fox_kernel, tq=tq, tk=tq),
        grid=(bsz, hp, s // tq),
        in_specs=[row,
                  pl.BlockSpec((1, 1, tq, nhs), lambda b, h, t: (b, h, t, 0)),
                  pl.BlockSpec((1, nhs, s, aug), lambda b, h, t: (b, h, 0, 0)),
                  pl.BlockSpec((1, nhs, dh, s), lambda b, h, t: (b, h, 0, 0)),
                  row],
        out_specs=row,
        out_shape=jax.ShapeDtypeStruct((bsz, s, nh * dh), BF16),
        scratch_shapes=[pltpu.VMEM((nhs, dh, tq), F32), pltpu.VMEM((nhs, tq, aug), BF16)],
        compiler_params=_params("parallel", "parallel", "arbitrary"),
        name="fox_attention",
    )(q, cq, ka, vt, gate)


def _fox_prep_kernel(k_ref, v_ref, c_ref, ka_ref, vt_ref):
    nh, dh = vt_ref.shape[1], vt_ref.shape[2]
    c3 = _split3_f32(c_ref[0])
    for h in range(nh):
        aug = _aug_cols(c3, h, ka_ref.shape[3] - dh, 0)
        ka_ref[0, h] = jnp.concatenate([k_ref[0, :, h * dh:(h + 1) * dh].astype(F32), aug], axis=1).astype(BF16)
    per = LANES // dh
    for g in range(nh // per):
        vt2 = v_ref[0, :, g * LANES:(g + 1) * LANES].astype(F32).T
        for j in range(per):
            vt_ref[0, g * per + j] = vt2[j * dh:(j + 1) * dh].astype(BF16)


def fox_prep_kv(k_sh, v_sh, c, *, ts=256):
    bsz, s, hd = k_sh.shape
    nh, dh = FOX_HEADS, FOX_HEAD_DIM
    row = pl.BlockSpec((1, ts, hd), lambda b, t: (b, t, 0))
    return pl.pallas_call(
        _fox_prep_kernel,
        grid=(bsz, s // ts),
        in_specs=[row, row, pl.BlockSpec((1, ts, LANES), lambda b, t: (b, t, 0))],
        out_specs=[pl.BlockSpec((1, nh, ts, LANES), lambda b, t: (b, 0, t, 0)),
                   pl.BlockSpec((1, nh, dh, ts), lambda b, t: (b, 0, 0, t))],
        out_shape=[jax.ShapeDtypeStruct((bsz, nh, s, LANES), BF16),
                   jax.ShapeDtypeStruct((bsz, nh, dh, s), BF16)],
        compiler_params=_params("parallel", "parallel"),
        name="fox_prep_kv",
    )(k_sh, v_sh, c)


def kernel(x, hg_w_in, hg_lb_logits, hg_norm_w, hg_w_out, kv_w, kv_fgate_b, fox_w_qg, fox_w_out, ln_mix_g, ln_mix_b, ln_ffn_g, ln_ffn_b, ffn_w_gu, ffn_w_down, moe_w_router, moe_w_gu, moe_w_down):
    bsz, s, d = x.shape
    depth = ln_mix_g.shape[0]
    n_a = hg_w_in.shape[0]
    alpha = (2 * depth) ** 0.25
    fox_dim = FOX_HEADS * FOX_HEAD_DIM
    t = bsz * s

    p = jax.nn.softmax(hg_lb_logits.astype(F32), axis=0)
    lbs = jnp.cumsum(p, axis=0) - p[0]

    def vec(a):
        return a.reshape(1, -1).astype(F32)

    x2 = x.reshape(t, d)
    ka = vt = cq = None
    for layer in range(depth):
        if layer < n_a:
            w_in = hg_w_in[layer].astype(BF16)
            n = w_in.shape[1] // 4
            q = mm(x2, w_in[:, :n], kind="silu")
            lf, kk = hgrn_gate(x2, w_in[:, n:2 * n], vec(lbs[layer]))
            v = mm(x2, w_in[:, 2 * n:3 * n])
            g = mm(x2, w_in[:, 3 * n:], kind="silu")
            sh = (bsz, s, -1)
            o = hgrn_recurrence(q.reshape(sh), kk.reshape(sh), v.reshape(sh), lf.reshape(sh), g.reshape(sh),
                                vec(hg_norm_w[layer]))
            x2 = mm_ln(o.reshape(t, -1), hg_w_out[layer].astype(BF16), x2, vec(ln_mix_g[layer]),
                       vec(ln_mix_b[layer]), alpha=alpha)
        else:
            bi = layer - n_a
            w_qg = fox_w_qg[bi].astype(BF16)
            q = mm(x2, w_qg[:, :fox_dim], kind="scale", scale=FOX_HEAD_DIM ** -0.5)
            gate = mm(x2, w_qg[:, fox_dim:], kind="sigmoid")
            sh = (bsz, s, fox_dim)
            o = fox_attention(q.reshape(sh), cq, ka, vt, gate.reshape(sh), nhs=FOX_HEADS_PER_STEP)
            x2 = mm_ln(o.reshape(t, fox_dim), fox_w_out[bi].astype(BF16), x2, vec(ln_mix_g[layer]),
                       vec(ln_mix_b[layer]), alpha=alpha)
        if layer % 2 == 0:
            x2 = ffn_ln(x2, ffn_w_gu[layer // 2].astype(BF16), ffn_w_down[layer // 2].astype(BF16),
                        vec(ln_ffn_g[layer]), vec(ln_ffn_b[layer]), alpha=alpha)
        else:
            w_r = jnp.pad(moe_w_router[layer // 2].astype(F32), ((0, 0), (0, LANES - N_EXPERTS)))
            w_r_hi = w_r.astype(BF16)
            w_r_lo = (w_r - w_r_hi.astype(F32)).astype(BF16)
            x2 = moe_ln(x2, w_r_hi, w_r_lo, moe_w_gu[layer // 2].astype(BF16), moe_w_down[layer // 2].astype(BF16),
                        vec(ln_ffn_g[layer]), vec(ln_ffn_b[layer]), alpha=alpha)
        if layer == n_a - 1:
            w_kv = kv_w.astype(BF16)
            k_sh = mm(x2, w_kv[:, :fox_dim]).reshape(bsz, s, fox_dim)
            v_sh = mm(x2, w_kv[:, fox_dim:2 * fox_dim]).reshape(bsz, s, fox_dim)
            w_f = jnp.pad(w_kv[:, 2 * fox_dim:], ((0, 0), (0, LANES - FOX_HEADS)))
            b_f = jnp.pad(kv_fgate_b.astype(F32), (0, LANES - FOX_HEADS)).reshape(1, LANES)
            log_f = mm(x2, w_f, kind="log_sigmoid_bias", bias=b_f, out_dtype=F32)
            c = seq_cumsum(log_f.reshape(bsz, s, LANES))
            ka, vt = fox_prep_kv(k_sh, v_sh, c)
            cq = c[:, :, :FOX_HEADS].reshape(bsz, s, FOX_HEADS // FOX_HEADS_PER_STEP, FOX_HEADS_PER_STEP)
            cq = cq.transpose(0, 2, 1, 3)
    return x2.reshape(bsz, s, d)
```

```python
import functools

import jax
import jax.numpy as jnp
from jax import lax
from jax.experimental import pallas as pl
from jax.experimental.pallas import tpu as pltpu

F32 = jnp.float32
BF16 = jnp.bfloat16

HG_HEADS = 8
HG_CHUNK = 64
HG_SUB = 8
LB_FLOOR = 1e-30
FOX_HEADS = 16
FOX_HEAD_DIM = 64
FOX_HEADS_PER_STEP = 8
N_EXPERTS = 8
TOP_K = 2
LN_EPS = 1e-5
RMS_EPS = 1e-6

LANES = 128
VMEM_LIMIT_BYTES = 56 * 1024 * 1024

NEG_BIG = -1e30
LOG2E = 1.4426950408889634


def _params(*sem):
    return pltpu.CompilerParams(dimension_semantics=sem, vmem_limit_bytes=VMEM_LIMIT_BYTES)


def _layer_norm(h, g, b):
    mu = jnp.mean(h, axis=-1, keepdims=True)
    d = h - mu
    var = jnp.mean(d * d, axis=-1, keepdims=True)
    return d * lax.rsqrt(var + LN_EPS) * g + b


def _silu(x):
    return x * jax.nn.sigmoid(x)


def _log1p_exp_neg(d):
    return jnp.log(1.0 + jnp.exp(-d))


def _log_sigmoid(z):
    return jnp.minimum(z, 0.0) - _log1p_exp_neg(jnp.abs(z))


def _split3(x):
    hi = x.astype(BF16)
    r1 = x - hi.astype(F32)
    mid = r1.astype(BF16)
    lo = (r1 - mid.astype(F32)).astype(BF16)
    return hi, mid, lo


def _tri(n):
    r = lax.broadcasted_iota(jnp.int32, (n, n), 0)
    c = lax.broadcasted_iota(jnp.int32, (n, n), 1)
    return (r >= c).astype(BF16)


def _cumsum_rows(x, tri):
    hi, mid, lo = _split3(x)
    return (jnp.dot(tri, hi, preferred_element_type=F32)
            + jnp.dot(tri, mid, preferred_element_type=F32)
            + jnp.dot(tri, lo, preferred_element_type=F32))


def _hgrn_gate_kernel(x_ref, w_ref, lb_ref, lf_ref, k_ref):
    z = jnp.dot(x_ref[...].astype(BF16), w_ref[...], preferred_element_type=F32)
    lb = lb_ref[...]
    log_lb = jnp.log(jnp.maximum(lb, LB_FLOOR))
    a = _log_sigmoid(z)
    b = log_lb + _log_sigmoid(-z)
    lf_ref[...] = jnp.maximum(a, b) + _log1p_exp_neg(jnp.abs(a - b))
    k_ref[...] = ((1.0 - lb) * jax.nn.sigmoid(-z)).astype(k_ref.dtype)


def hgrn_gate(x2, w_f, lb, *, tm=512):
    m, d = x2.shape
    n = w_f.shape[1]
    row = lambda i: (i, 0)
    fixed = lambda i: (0, 0)
    return pl.pallas_call(
        _hgrn_gate_kernel,
        grid=(m // tm,),
        in_specs=[pl.BlockSpec((tm, d), row), pl.BlockSpec((d, n), fixed), pl.BlockSpec((1, n), fixed)],
        out_specs=[pl.BlockSpec((tm, n), row)] * 2,
        out_shape=[jax.ShapeDtypeStruct((m, n), F32), jax.ShapeDtypeStruct((m, n), BF16)],
        compiler_params=_params("parallel"),
        name="hgrn_gate",
    )(x2, w_f, lb)


def _hgrn_chunk(qs, ks, vs, lfs, sts, tri):
    heads = range(len(qs))
    c = qs[0].shape[0]
    nblk = c // HG_SUB
    nt = (((1,), (1,)), ((), ()))
    tn = (((0,), (0,)), ((), ()))
    bs = [_cumsum_rows(lfs[h], tri) for h in heads]
    o_in = [lax.dot_general((qs[h] * jnp.exp(bs[h])).astype(BF16), sts[h].astype(BF16), nt,
                            preferred_element_type=F32) for h in heads]
    b_last = [bs[h][c - 1:c] for h in heads]
    upd = [lax.dot_general(vs[h].astype(BF16), (ks[h] * jnp.exp(b_last[h] - bs[h])).astype(BF16), tn,
                           preferred_element_type=F32) for h in heads]
    st_new = [sts[h] * jnp.exp(b_last[h]) + upd[h] for h in heads]

    p = {}
    for i in range(1, nblk):
        lo, hi = i * HG_SUB, (i + 1) * HG_SUB
        for h in heads:
            r = bs[h][lo - 1:lo]
            qd = qs[h][lo:hi] * jnp.exp(bs[h][lo:hi] - r)
            kp = ks[h][:lo] * jnp.exp(r - bs[h][:lo])
            p[i, h] = lax.dot_general(qd.astype(BF16), kp.astype(BF16), nt, preferred_element_type=F32)
    off = {}
    for i in range(1, nblk):
        lo = i * HG_SUB
        for h in heads:
            off[i, h] = jnp.dot(p[i, h].astype(BF16), vs[h][:lo].astype(BF16), preferred_element_type=F32)

    row = lax.broadcasted_iota(jnp.int32, (HG_SUB, 1), 0)
    outs = []
    for h in heads:
        blocks = []
        for i in range(nblk):
            lo, hi = i * HG_SUB, (i + 1) * HG_SUB
            bi, qi, ki, vi = bs[h][lo:hi], qs[h][lo:hi], ks[h][lo:hi], vs[h][lo:hi]
            oi = o_in[h][lo:hi]
            if i > 0:
                oi = oi + off[i, h]
            for s in range(HG_SUB):
                e = jnp.exp(bi - bi[s:s + 1])
                a = jnp.sum(qi * e * ki[s:s + 1], axis=-1, keepdims=True)
                a = jnp.where(row >= s, a, 0.0)
                oi = oi + a * vi[s:s + 1]
            blocks.append(oi)
        outs.append(jnp.concatenate(blocks, axis=0))
    return outs, st_new


def _hgrn_rec_kernel(q_ref, k_ref, v_ref, lf_ref, g_ref, nw_ref, o_ref, st_ref, *, n_chunks, dh):
    nh = st_ref.shape[0]

    @pl.when(pl.program_id(2) == 0)
    def _():
        st_ref[...] = jnp.zeros_like(st_ref)

    tri = _tri(HG_CHUNK)
    nw = nw_ref[...]

    def body(ci, carry):
        sl = pl.ds(pl.multiple_of(ci * HG_CHUNK, HG_CHUNK), HG_CHUNK)
        cols = [slice(h * dh, (h + 1) * dh) for h in range(nh)]
        outs, st_new = _hgrn_chunk([q_ref[0, sl, cs].astype(F32) for cs in cols],
                                   [k_ref[0, sl, cs].astype(F32) for cs in cols],
                                   [v_ref[0, sl, cs].astype(F32) for cs in cols], [lf_ref[0, sl, cs] for cs in cols],
                                   [st_ref[h] for h in range(nh)], tri)
        for h in range(nh):
            st_ref[h] = st_new[h]
            o = outs[h]
            o = o * lax.rsqrt(jnp.mean(o * o, axis=-1, keepdims=True) + RMS_EPS)
            o = o * nw * g_ref[0, sl, cols[h]].astype(F32)
            o_ref[0, sl, cols[h]] = o.astype(o_ref.dtype)
        return carry

    lax.fori_loop(0, n_chunks, body, 0)


def hgrn_recurrence(q, k, v, lf, g, norm_w, *, sblk=512, nhs=4):
    bsz, s, hd = q.shape
    dh = hd // HG_HEADS
    blk = pl.BlockSpec((1, sblk, nhs * dh), lambda b, h, t: (b, t, h))
    return pl.pallas_call(
        functools.partial(_hgrn_rec_kernel, n_chunks=sblk // HG_CHUNK, dh=dh),
        grid=(bsz, HG_HEADS // nhs, s // sblk),
        in_specs=[blk, blk, blk, blk, blk, pl.BlockSpec((1, dh), lambda b, h, t: (0, 0))],
        out_specs=blk,
        out_shape=jax.ShapeDtypeStruct((bsz, s, hd), BF16),
        scratch_shapes=[pltpu.VMEM((nhs, dh, dh), F32)],
        compiler_params=_params("parallel", "parallel", "arbitrary"),
        name="hgrn_recurrence",
    )(q, k, v, lf, g, norm_w)


def _mm_ln_kernel(h_ref, w_ref, x_ref, g_ref, b_ref, o_ref, *, alpha):
    y = jnp.dot(h_ref[...], w_ref[...], preferred_element_type=F32)
    o_ref[...] = _layer_norm(alpha * x_ref[...] + y, g_ref[...], b_ref[...])


def mm_ln(h, w, x2, g, b, *, alpha, tm=512):
    m, kdim = h.shape
    d = w.shape[1]
    row = lambda i: (i, 0)
    fixed = lambda i: (0, 0)
    return pl.pallas_call(
        functools.partial(_mm_ln_kernel, alpha=alpha),
        grid=(m // tm,),
        in_specs=[pl.BlockSpec((tm, kdim), row), pl.BlockSpec((kdim, d), fixed),
                  pl.BlockSpec((tm, d), row), pl.BlockSpec((1, d), fixed), pl.BlockSpec((1, d), fixed)],
        out_specs=pl.BlockSpec((tm, d), row),
        out_shape=jax.ShapeDtypeStruct((m, d), F32),
        compiler_params=_params("parallel"),
        name="mm_ln",
    )(h, w, x2, g, b)


def _ffn_ln_kernel(x_ref, wg_ref, wu_ref, wd_ref, g_ref, b_ref, o_ref, *, alpha):
    xb = x_ref[...].astype(BF16)
    a = jnp.dot(xb, wg_ref[...], preferred_element_type=F32)
    u = jnp.dot(xb, wu_ref[...], preferred_element_type=F32)
    y = jnp.dot((_silu(a) * u).astype(BF16), wd_ref[...], preferred_element_type=F32)
    o_ref[...] = _layer_norm(alpha * x_ref[...] + y, g_ref[...], b_ref[...])


def ffn_ln(x2, w_gu, w_down, g, b, *, alpha, tm=512):
    m, d = x2.shape
    f = w_down.shape[0]
    row = lambda i: (i, 0)
    fixed = lambda i: (0, 0)
    once = pl.Buffered(1)
    return pl.pallas_call(
        functools.partial(_ffn_ln_kernel, alpha=alpha),
        grid=(m // tm,),
        in_specs=[pl.BlockSpec((tm, d), row),
                  pl.BlockSpec((d, f), lambda i: (0, 0), pipeline_mode=once),
                  pl.BlockSpec((d, f), lambda i: (0, 1), pipeline_mode=once),
                  pl.BlockSpec((f, d), fixed, pipeline_mode=once),
                  pl.BlockSpec((1, d), fixed), pl.BlockSpec((1, d), fixed)],
        out_specs=pl.BlockSpec((tm, d), row),
        out_shape=jax.ShapeDtypeStruct((m, d), F32),
        compiler_params=_params("parallel"),
        name="ffn_ln",
    )(x2, w_gu, w_gu, w_down, g, b)


def _router_kernel(x_ref, wh_ref, wl_ref, o_ref):
    x = x_ref[...]
    xh = x.astype(BF16)
    xl = (x - xh.astype(F32)).astype(BF16)
    wh, wl = wh_ref[...], wl_ref[...]
    logits = (jnp.dot(xh, wh, preferred_element_type=F32)
              + (jnp.dot(xl, wh, preferred_element_type=F32) + jnp.dot(xh, wl, preferred_element_type=F32)))
    lane = lax.broadcasted_iota(jnp.int32, logits.shape, 1)
    logits = jnp.where(lane < N_EXPERTS, logits, -jnp.inf)
    m1 = jnp.max(logits, axis=-1, keepdims=True)
    i1 = jnp.min(jnp.where(logits == m1, lane, LANES), axis=-1, keepdims=True)
    rest = jnp.where(lane == i1, -jnp.inf, logits)
    m2 = jnp.max(rest, axis=-1, keepdims=True)
    i2 = jnp.min(jnp.where(rest == m2, lane, LANES), axis=-1, keepdims=True)
    e2 = jnp.exp(m2 - m1)
    den = 1.0 + e2
    w1 = 1.0 / den
    w2 = e2 / den
    o_ref[...] = jnp.where(lane == 0, i1.astype(F32),
                           jnp.where(lane == 1, i2.astype(F32),
                                     jnp.where(lane == 2, w1, jnp.where(lane == 3, w2, 0.0))))


def router(x2, w_hi, w_lo, *, tm=1024):
    m, d = x2.shape
    row = lambda i: (i, 0)
    fixed = lambda i: (0, 0)
    return pl.pallas_call(
        _router_kernel,
        grid=(m // tm,),
        in_specs=[pl.BlockSpec((tm, d), row), pl.BlockSpec((d, LANES), fixed), pl.BlockSpec((d, LANES), fixed)],
        out_specs=pl.BlockSpec((tm, LANES), row),
        out_shape=jax.ShapeDtypeStruct((m, LANES), F32),
        compiler_params=_params("parallel"),
        name="moe_router",
    )(x2, w_hi, w_lo)


def _row_copy(src_hbm, dst_vmem, sem, src_row, dst_row):
    return pltpu.make_async_copy(src_hbm.at[pl.ds(src_row, 1)], dst_vmem.at[pl.ds(dst_row, 1)], sem)


def _gather_start(idx_ref, base, src_hbm, dst_vmem, sem, n_rows):
    for r in range(n_rows):
        _row_copy(src_hbm, dst_vmem, sem, idx_ref[base + r], r).start(priority=r % 2)


def _gather_wait(src_hbm, dst_vmem, sem, n_rows):
    pltpu.make_async_copy(src_hbm.at[pl.ds(0, n_rows)], dst_vmem, sem).wait()


def _gather_kernel(idx_ref, src_hbm, o_ref, sem, *, rows):
    _gather_start(idx_ref, pl.program_id(0) * rows, src_hbm, o_ref, sem, rows)
    _gather_wait(src_hbm, o_ref, sem, rows)


def gather_rows(src, idx, *, rows=1024):
    n = idx.shape[0]
    d = src.shape[1]
    return pl.pallas_call(
        functools.partial(_gather_kernel, rows=rows),
        grid_spec=pltpu.PrefetchScalarGridSpec(
            num_scalar_prefetch=1,
            grid=(n // rows,),
            in_specs=[pl.BlockSpec(memory_space=pl.ANY)],
            out_specs=pl.BlockSpec((rows, d), lambda i, idx: (i, 0)),
            scratch_shapes=[pltpu.SemaphoreType.DMA(())]),
        out_shape=jax.ShapeDtypeStruct((n, d), src.dtype),
        compiler_params=_params("arbitrary"),
        name="moe_gather",
    )(idx, src)


def _experts_kernel(te_ref, nu_ref, xs_ref, wg_ref, wu_ref, wd_ref, o_ref, acc_ref):
    r, c = pl.program_id(0), pl.program_id(1)

    @pl.when(r < nu_ref[0])
    def _():
        xb = xs_ref[...].astype(BF16)
        a = jnp.dot(xb, wg_ref[0], preferred_element_type=F32)
        u = jnp.dot(xb, wu_ref[0], preferred_element_type=F32)
        part = jnp.dot((_silu(a) * u).astype(BF16), wd_ref[0], preferred_element_type=F32)

        @pl.when(c == 0)
        def _():
            acc_ref[...] = part

        @pl.when(c > 0)
        def _():
            acc_ref[...] += part

        @pl.when(c == pl.num_programs(1) - 1)
        def _():
            o_ref[...] = acc_ref[...]

    @pl.when(jnp.logical_and(r >= nu_ref[0], c == pl.num_programs(1) - 1))
    def _():
        o_ref[...] = jnp.zeros_like(o_ref)


def experts(xs, w_gu, w_down, tile_expert, n_used, *, tm, th=1792):
    n, d = xs.shape
    f = w_down.shape[1]
    nc = f // th
    return pl.pallas_call(
        _experts_kernel,
        grid_spec=pltpu.PrefetchScalarGridSpec(
            num_scalar_prefetch=2,
            grid=(n // tm, nc),
            in_specs=[pl.BlockSpec((tm, d), lambda r, c, te, nu: (r, 0)),
                      pl.BlockSpec((1, d, th), lambda r, c, te, nu: (te[r], 0, c)),
                      pl.BlockSpec((1, d, th), lambda r, c, te, nu: (te[r], 0, nc + c)),
                      pl.BlockSpec((1, th, d), lambda r, c, te, nu: (te[r], c, 0))],
            out_specs=pl.BlockSpec((tm, d), lambda r, c, te, nu: (r, 0)),
            scratch_shapes=[pltpu.VMEM((tm, d), F32)]),
        out_shape=jax.ShapeDtypeStruct((n, d), F32),
        compiler_params=_params("arbitrary", "arbitrary"),
        name="moe_experts",
    )(tile_expert, n_used, xs, w_gu, w_gu, w_down)


def _combine_ln_kernel(s1_ref, s2_ref, ys_hbm, x_ref, rw_ref, g_ref, b_ref, o_ref, a_buf, b_buf, sem, *,
                       alpha, rows):
    base = pl.program_id(0) * rows
    _gather_start(s1_ref, base, ys_hbm, a_buf, sem.at[0], rows)
    _gather_start(s2_ref, base, ys_hbm, b_buf, sem.at[1], rows)
    _gather_wait(ys_hbm, a_buf, sem.at[0], rows)
    _gather_wait(ys_hbm, b_buf, sem.at[1], rows)
    rw = rw_ref[...]
    y = rw[:, 2:3] * a_buf[...] + rw[:, 3:4] * b_buf[...]
    o_ref[...] = _layer_norm(alpha * x_ref[...] + y, g_ref[...], b_ref[...])


def combine_ln(ys, slot1, slot2, x2, rw, g, b, *, alpha, rows=512):
    m, d = x2.shape
    row = lambda i, s1, s2: (i, 0)
    fixed = lambda i, s1, s2: (0, 0)
    return pl.pallas_call(
        functools.partial(_combine_ln_kernel, alpha=alpha, rows=rows),
        grid_spec=pltpu.PrefetchScalarGridSpec(
            num_scalar_prefetch=2,
            grid=(m // rows,),
            in_specs=[pl.BlockSpec(memory_space=pl.ANY),
                      pl.BlockSpec((rows, d), row), pl.BlockSpec((rows, LANES), row),
                      pl.BlockSpec((1, d), fixed), pl.BlockSpec((1, d), fixed)],
            out_specs=pl.BlockSpec((rows, d), row),
            scratch_shapes=[pltpu.VMEM((rows, d), F32), pltpu.VMEM((rows, d), F32),
                            pltpu.SemaphoreType.DMA((2,))]),
        out_shape=jax.ShapeDtypeStruct((m, d), F32),
        compiler_params=_params("arbitrary"),
        name="moe_combine_ln",
    )(slot1, slot2, ys, x2, rw, g, b)


def moe_ln(x2, w_r_hi, w_r_lo, w_gu, w_down, g, b, *, alpha, tm=512):
    t = x2.shape[0]
    rw = router(x2, w_r_hi, w_r_lo)
    e_idx = rw[:, :TOP_K].astype(jnp.int32)
    flat_e = e_idx.reshape(-1)
    onehot = (flat_e[:, None] == jnp.arange(N_EXPERTS, dtype=jnp.int32)[None, :]).astype(jnp.int32)
    rank = jnp.take_along_axis(jnp.cumsum(onehot, axis=0) - onehot, flat_e[:, None], axis=1)[:, 0]
    counts = jnp.sum(onehot, axis=0)
    padded = ((counts + tm - 1) // tm) * tm
    ends = jnp.cumsum(padded)
    starts = ends - padded
    slot = starts[flat_e] + rank
    n_tiles = (TOP_K * t) // tm + N_EXPERTS
    n_slots = n_tiles * tm
    token = jnp.arange(TOP_K * t, dtype=jnp.int32) // TOP_K
    token_of_slot = jnp.zeros((n_slots,), jnp.int32).at[slot].set(token)
    tile_start = jnp.arange(n_tiles, dtype=jnp.int32) * tm
    tile_expert = jnp.minimum(jnp.sum((tile_start[:, None] >= ends[None, :]).astype(jnp.int32), axis=1),
                              N_EXPERTS - 1).astype(jnp.int32)
    n_used = (ends[-1:] // tm).astype(jnp.int32)

    xs = gather_rows(x2, token_of_slot)
    ys = experts(xs, w_gu, w_down, tile_expert, n_used, tm=tm)
    slot2d = slot.reshape(t, TOP_K).astype(jnp.int32)
    return combine_ln(ys, slot2d[:, 0], slot2d[:, 1], x2, rw, g, b, alpha=alpha)


def _mm_kernel(x_ref, w_ref, bias_ref, o_ref, *, kind, scale):
    acc = jnp.dot(x_ref[...].astype(BF16), w_ref[...], preferred_element_type=F32)
    if kind == "scale":
        acc = acc * scale
    elif kind == "sigmoid":
        acc = jax.nn.sigmoid(acc)
    elif kind == "silu":
        acc = _silu(acc)
    elif kind == "log_sigmoid_bias":
        acc = _log_sigmoid(acc + bias_ref[...])
    o_ref[...] = acc.astype(o_ref.dtype)


def mm(x2, w, *, kind="none", scale=1.0, bias=None, out_dtype=BF16, tm=1024, tn=None):
    m, d = x2.shape
    n = w.shape[1]
    tn = n if tn is None else tn
    if bias is None:
        bias = jnp.zeros((1, n), F32)
    return pl.pallas_call(
        functools.partial(_mm_kernel, kind=kind, scale=scale),
        grid=(m // tm, n // tn),
        in_specs=[pl.BlockSpec((tm, d), lambda i, j: (i, 0)), pl.BlockSpec((d, tn), lambda i, j: (0, j)),
                  pl.BlockSpec((1, tn), lambda i, j: (0, j))],
        out_specs=pl.BlockSpec((tm, tn), lambda i, j: (i, j)),
        out_shape=jax.ShapeDtypeStruct((m, n), out_dtype),
        compiler_params=_params("parallel", "arbitrary"),
        name="mm_" + kind,
    )(x2, w, bias)


def _seq_cumsum_kernel(x_ref, o_ref, *, blk):
    tri = _tri(blk)
    n = x_ref.shape[1] // blk

    def body(i, carry):
        sl = pl.ds(pl.multiple_of(i * blk, blk), blk)
        c = _cumsum_rows(x_ref[0, sl, :], tri) + carry
        o_ref[0, sl, :] = c
        return c[blk - 1:blk]

    lax.fori_loop(0, n, body, jnp.zeros((1, x_ref.shape[2]), F32))


def seq_cumsum(x, *, blk=256):
    bsz, s, n = x.shape
    spec = pl.BlockSpec((1, s, n), lambda b: (b, 0, 0))
    return pl.pallas_call(
        functools.partial(_seq_cumsum_kernel, blk=blk),
        grid=(bsz,), in_specs=[spec], out_specs=spec,
        out_shape=jax.ShapeDtypeStruct(x.shape, F32),
        compiler_params=_params("parallel"),
        name="seq_cumsum",
    )(x)


def _split3_f32(c):
    return tuple(t.astype(F32) for t in _split3(c))


def _aug_cols(c3, col, width, first):
    hi, mid, lo = (t[:, col:col + 1] for t in c3)
    lane = lax.broadcasted_iota(jnp.int32, (hi.shape[0], width), 1)
    fill = jnp.where(lane < 6, 1.0 if first == 0 else -1.0, 0.0)
    return jnp.where(lane == first, hi, jnp.where(lane == first + 1, mid, jnp.where(lane == first + 2, lo, fill)))


def _fox_kernel(q_ref, cq_ref, ka_ref, vt_ref, gate_ref, o_ref, acc_ref, qt_ref, *, tq, tk):
    qi = pl.program_id(2)
    nhs, dh = acc_ref.shape[0], acc_ref.shape[1]
    rowi = lax.broadcasted_iota(jnp.int32, (tk, tq), 0)
    coli = lax.broadcasted_iota(jnp.int32, (tk, tq), 1)
    acc_ref[...] = jnp.zeros_like(acc_ref)
    c3 = _split3_f32(cq_ref[0, 0] * LOG2E)
    for j in range(nhs):
        qa = jnp.concatenate([q_ref[0, :, j * dh:(j + 1) * dh].astype(F32), _aug_cols(c3, j, dh, 3)], axis=1)
        qt_ref[j] = qa.astype(BF16)

    def tile(ki, stats, masked):
        ks = pl.ds(pl.multiple_of(ki * tk, tk), tk)
        heads = range(len(stats))
        nt = (((1,), (1,)), ((), ()))
        ss = [lax.dot_general(ka_ref[0, j, ks, :], qt_ref[j], nt, preferred_element_type=F32) for j in heads]
        if masked:
            ss = [jnp.where(rowi <= coli, s, NEG_BIG) for s in ss]
        new, alphas, ps = [], [], []
        for j in heads:
            m_old, l_old = stats[j]
            m_new = jnp.maximum(m_old, jnp.max(ss[j], axis=0, keepdims=True))
            alpha = jnp.exp2(m_old - m_new)
            p = jnp.exp2(ss[j] - m_new)
            new.append((m_new, alpha * l_old + jnp.sum(p, axis=0, keepdims=True)))
            alphas.append(alpha)
            ps.append(p.astype(BF16))
        pvs = [jnp.dot(vt_ref[0, j, :, ks], ps[j], preferred_element_type=F32) for j in heads]
        for j in heads:
            acc_ref[j] = alphas[j] * acc_ref[j] + pvs[j]
        return tuple(new)

    init = tuple((jnp.full((1, tq), NEG_BIG, F32), jnp.zeros((1, tq), F32)) for _ in range(nhs))
    stats = lax.fori_loop(0, qi, lambda ki, st: tile(ki, st, False), init)
    stats = tile(qi, stats, True)
    ot = jnp.concatenate([acc_ref[j] / stats[j][1] for j in range(nhs)], axis=0)
    o_ref[0] = (ot.T * gate_ref[0].astype(F32)).astype(o_ref.dtype)


def fox_attention(q, cq, ka, vt, gate, *, tq=256, nhs=8):
    bsz, nh, dh, s = vt.shape
    aug = ka.shape[3]
    hp = nh // nhs
    row = pl.BlockSpec((1, tq, nhs * dh), lambda b, h, t: (b, t, h))
    return pl.pallas_call(
        functools.partial(_fox_kernel, tq=tq, tk=tq),
        grid=(bsz, hp, s // tq),
        in_specs=[row,
                  pl.BlockSpec((1, 1, tq, nhs), lambda b, h, t: (b, h, t, 0)),
                  pl.BlockSpec((1, nhs, s, aug), lambda b, h, t: (b, h, 0, 0)),
                  pl.BlockSpec((1, nhs, dh, s), lambda b, h, t: (b, h, 0, 0)),
                  row],
        out_specs=row,
        out_shape=jax.ShapeDtypeStruct((bsz, s, nh * dh), BF16),
        scratch_shapes=[pltpu.VMEM((nhs, dh, tq), F32), pltpu.VMEM((nhs, tq, aug), BF16)],
        compiler_params=_params("parallel", "parallel", "arbitrary"),
        name="fox_attention",
    )(q, cq, ka, vt, gate)


def _fox_prep_kernel(k_ref, v_ref, c_ref, ka_ref, vt_ref):
    nh, dh = vt_ref.shape[1], vt_ref.shape[2]
    c3 = _split3_f32(c_ref[0] * LOG2E)
    for h in range(nh):
        aug = _aug_cols(c3, h, ka_ref.shape[3] - dh, 0)
        ka_ref[0, h] = jnp.concatenate([k_ref[0, :, h * dh:(h + 1) * dh].astype(F32), aug], axis=1).astype(BF16)
    per = LANES // dh
    for g in range(nh // per):
        vt2 = v_ref[0, :, g * LANES:(g + 1) * LANES].astype(F32).T
        for j in range(per):
            vt_ref[0, g * per + j] = vt2[j * dh:(j + 1) * dh].astype(BF16)


def fox_prep_kv(k_sh, v_sh, c, *, ts=256):
    bsz, s, hd = k_sh.shape
    nh, dh = FOX_HEADS, FOX_HEAD_DIM
    row = pl.BlockSpec((1, ts, hd), lambda b, t: (b, t, 0))
    return pl.pallas_call(
        _fox_prep_kernel,
        grid=(bsz, s // ts),
        in_specs=[row, row, pl.BlockSpec((1, ts, LANES), lambda b, t: (b, t, 0))],
        out_specs=[pl.BlockSpec((1, nh, ts, LANES), lambda b, t: (b, 0, t, 0)),
                   pl.BlockSpec((1, nh, dh, ts), lambda b, t: (b, 0, 0, t))],
        out_shape=[jax.ShapeDtypeStruct((bsz, nh, s, LANES), BF16),
                   jax.ShapeDtypeStruct((bsz, nh, dh, s), BF16)],
        compiler_params=_params("parallel", "parallel"),
        name="fox_prep_kv",
    )(k_sh, v_sh, c)


def kernel(x, hg_w_in, hg_lb_logits, hg_norm_w, hg_w_out, kv_w, kv_fgate_b, fox_w_qg, fox_w_out, ln_mix_g, ln_mix_b, ln_ffn_g, ln_ffn_b, ffn_w_gu, ffn_w_down, moe_w_router, moe_w_gu, moe_w_down):
    bsz, s, d = x.shape
    depth = ln_mix_g.shape[0]
    n_a = hg_w_in.shape[0]
    alpha = (2 * depth) ** 0.25
    fox_dim = FOX_HEADS * FOX_HEAD_DIM
    t = bsz * s

    p = jax.nn.softmax(hg_lb_logits.astype(F32), axis=0)
    lbs = jnp.cumsum(p, axis=0) - p[0]

    def vec(a):
        return a.reshape(1, -1).astype(F32)

    x2 = x.reshape(t, d)
    ka = vt = cq = None
    for layer in range(depth):
        if layer < n_a:
            w_in = hg_w_in[layer].astype(BF16)
            n = w_in.shape[1] // 4
            q = mm(x2, w_in[:, :n], kind="silu")
            lf, kk = hgrn_gate(x2, w_in[:, n:2 * n], vec(lbs[layer]))
            v = mm(x2, w_in[:, 2 * n:3 * n])
            g = mm(x2, w_in[:, 3 * n:], kind="silu")
            sh = (bsz, s, -1)
            o = hgrn_recurrence(q.reshape(sh), kk.reshape(sh), v.reshape(sh), lf.reshape(sh), g.reshape(sh),
                                vec(hg_norm_w[layer]))
            x2 = mm_ln(o.reshape(t, -1), hg_w_out[layer].astype(BF16), x2, vec(ln_mix_g[layer]),
                       vec(ln_mix_b[layer]), alpha=alpha)
        else:
            bi = layer - n_a
            w_qg = fox_w_qg[bi].astype(BF16)
            q = mm(x2, w_qg[:, :fox_dim], kind="scale", scale=FOX_HEAD_DIM ** -0.5 * LOG2E)
            gate = mm(x2, w_qg[:, fox_dim:], kind="sigmoid")
            sh = (bsz, s, fox_dim)
            o = fox_attention(q.reshape(sh), cq, ka, vt, gate.reshape(sh), nhs=FOX_HEADS_PER_STEP)
            x2 = mm_ln(o.reshape(t, fox_dim), fox_w_out[bi].astype(BF16), x2, vec(ln_mix_g[layer]),
                       vec(ln_mix_b[layer]), alpha=alpha)
        if layer % 2 == 0:
            x2 = ffn_ln(x2, ffn_w_gu[layer // 2].astype(BF16), ffn_w_down[layer // 2].astype(BF16),
                        vec(ln_ffn_g[layer]), vec(ln_ffn_b[layer]), alpha=alpha)
        else:
            w_r = jnp.pad(moe_w_router[layer // 2].astype(F32), ((0, 0), (0, LANES - N_EXPERTS)))
            w_r_hi = w_r.astype(BF16)
            w_r_lo = (w_r - w_r_hi.astype(F32)).astype(BF16)
            x2 = moe_ln(x2, w_r_hi, w_r_lo, moe_w_gu[layer // 2].astype(BF16), moe_w_down[layer // 2].astype(BF16),
                        vec(ln_ffn_g[layer]), vec(ln_ffn_b[layer]), alpha=alpha)
        if layer == n_a - 1:
            w_kv = kv_w.astype(BF16)
            k_sh = mm(x2, w_kv[:, :fox_dim]).reshape(bsz, s, fox_dim)
            v_sh = mm(x2, w_kv[:, fox_dim:2 * fox_dim]).reshape(bsz, s, fox_dim)
            w_f = jnp.pad(w_kv[:, 2 * fox_dim:], ((0, 0), (0, LANES - FOX_HEADS)))
            b_f = jnp.pad(kv_fgate_b.astype(F32), (0, LANES - FOX_HEADS)).reshape(1, LANES)
            log_f = mm(x2, w_f, kind="log_sigmoid_bias", bias=b_f, out_dtype=F32)
            c = seq_cumsum(log_f.reshape(bsz, s, LANES))
            ka, vt = fox_prep_kv(k_sh, v_sh, c)
            cq = c[:, :, :FOX_HEADS].reshape(bsz, s, FOX_HEADS // FOX_HEADS_PER_STEP, FOX_HEADS_PER_STEP)
            cq = cq.transpose(0, 2, 1, 3)
    return x2.reshape(bsz, s, d)
```

```python
import functools

import jax
import jax.numpy as jnp
from jax import lax
from jax.experimental import pallas as pl
from jax.experimental.pallas import tpu as pltpu

F32 = jnp.float32
BF16 = jnp.bfloat16

HG_HEADS = 8
HG_CHUNK = 64
HG_SUB = 8
LB_FLOOR = 1e-30
FOX_HEADS = 16
FOX_HEAD_DIM = 64
FOX_HEADS_PER_STEP = 8
N_EXPERTS = 8
TOP_K = 2
LN_EPS = 1e-5
RMS_EPS = 1e-6

LANES = 128
VMEM_LIMIT_BYTES = 56 * 1024 * 1024

NEG_BIG = -1e30
LOG2E = 1.4426950408889634


def _params(*sem):
    return pltpu.CompilerParams(dimension_semantics=sem, vmem_limit_bytes=VMEM_LIMIT_BYTES)


def _layer_norm(h, g, b):
    mu = jnp.mean(h, axis=-1, keepdims=True)
    d = h - mu
    var = jnp.mean(d * d, axis=-1, keepdims=True)
    return d * lax.rsqrt(var + LN_EPS) * g + b


def _silu(x):
    return x * jax.nn.sigmoid(x)


def _log1p_exp_neg(d):
    return jnp.log(1.0 + jnp.exp(-d))


def _log_sigmoid(z):
    return jnp.minimum(z, 0.0) - _log1p_exp_neg(jnp.abs(z))


def _split3(x):
    hi = x.astype(BF16)
    r1 = x - hi.astype(F32)
    mid = r1.astype(BF16)
    lo = (r1 - mid.astype(F32)).astype(BF16)
    return hi, mid, lo


def _tri(n):
    r = lax.broadcasted_iota(jnp.int32, (n, n), 0)
    c = lax.broadcasted_iota(jnp.int32, (n, n), 1)
    return (r >= c).astype(BF16)


def _cumsum_rows(x, tri):
    hi, mid, lo = _split3(x)
    return (jnp.dot(tri, hi, preferred_element_type=F32)
            + jnp.dot(tri, mid, preferred_element_type=F32)
            + jnp.dot(tri, lo, preferred_element_type=F32))


def _hgrn_gate_kernel(x_ref, w_ref, lb_ref, lf_ref, k_ref):
    z = jnp.dot(x_ref[...].astype(BF16), w_ref[...], preferred_element_type=F32)
    lb = lb_ref[...]
    log_lb = jnp.log(jnp.maximum(lb, LB_FLOOR))
    a = _log_sigmoid(z)
    b = log_lb + _log_sigmoid(-z)
    lf_ref[...] = jnp.maximum(a, b) + _log1p_exp_neg(jnp.abs(a - b))
    k_ref[...] = ((1.0 - lb) * jax.nn.sigmoid(-z)).astype(k_ref.dtype)


def hgrn_gate(x2, w_f, lb, *, tm=512):
    m, d = x2.shape
    n = w_f.shape[1]
    row = lambda i: (i, 0)
    fixed = lambda i: (0, 0)
    return pl.pallas_call(
        _hgrn_gate_kernel,
        grid=(m // tm,),
        in_specs=[pl.BlockSpec((tm, d), row), pl.BlockSpec((d, n), fixed), pl.BlockSpec((1, n), fixed)],
        out_specs=[pl.BlockSpec((tm, n), row)] * 2,
        out_shape=[jax.ShapeDtypeStruct((m, n), F32), jax.ShapeDtypeStruct((m, n), BF16)],
        compiler_params=_params("parallel"),
        name="hgrn_gate",
    )(x2, w_f, lb)


def _hgrn_chunk(qs, ks, vs, lfs, sts, tri):
    heads = range(len(qs))
    c = qs[0].shape[0]
    nblk = c // HG_SUB
    nt = (((1,), (1,)), ((), ()))
    tn = (((0,), (0,)), ((), ()))
    bs = [_cumsum_rows(lfs[h], tri) for h in heads]
    o_in = [lax.dot_general((qs[h] * jnp.exp(bs[h])).astype(BF16), sts[h].astype(BF16), nt,
                            preferred_element_type=F32) for h in heads]
    b_last = [bs[h][c - 1:c] for h in heads]
    upd = [lax.dot_general(vs[h].astype(BF16), (ks[h] * jnp.exp(b_last[h] - bs[h])).astype(BF16), tn,
                           preferred_element_type=F32) for h in heads]
    st_new = [sts[h] * jnp.exp(b_last[h]) + upd[h] for h in heads]

    p = {}
    for i in range(1, nblk):
        lo, hi = i * HG_SUB, (i + 1) * HG_SUB
        for h in heads:
            r = bs[h][lo - 1:lo]
            qd = qs[h][lo:hi] * jnp.exp(bs[h][lo:hi] - r)
            kp = ks[h][:lo] * jnp.exp(r - bs[h][:lo])
            p[i, h] = lax.dot_general(qd.astype(BF16), kp.astype(BF16), nt, preferred_element_type=F32)
    off = {}
    for i in range(1, nblk):
        lo = i * HG_SUB
        for h in heads:
            off[i, h] = jnp.dot(p[i, h].astype(BF16), vs[h][:lo].astype(BF16), preferred_element_type=F32)

    row = lax.broadcasted_iota(jnp.int32, (HG_SUB, 1), 0)
    outs = []
    for h in heads:
        blocks = []
        for i in range(nblk):
            lo, hi = i * HG_SUB, (i + 1) * HG_SUB
            bi, qi, ki, vi = bs[h][lo:hi], qs[h][lo:hi], ks[h][lo:hi], vs[h][lo:hi]
            oi = o_in[h][lo:hi]
            if i > 0:
                oi = oi + off[i, h]
            for s in range(HG_SUB):
                e = jnp.exp(bi - bi[s:s + 1])
                a = jnp.sum(qi * e * ki[s:s + 1], axis=-1, keepdims=True)
                a = jnp.where(row >= s, a, 0.0)
                oi = oi + a * vi[s:s + 1]
            blocks.append(oi)
        outs.append(jnp.concatenate(blocks, axis=0))
    return outs, st_new


def _hgrn_rec_kernel(q_ref, k_ref, v_ref, lf_ref, g_ref, nw_ref, o_ref, st_ref, *, n_chunks, dh):
    nh = st_ref.shape[0]

    @pl.when(pl.program_id(2) == 0)
    def _():
        st_ref[...] = jnp.zeros_like(st_ref)

    tri = _tri(HG_CHUNK)
    nw = nw_ref[...]

    def body(ci, carry):
        sl = pl.ds(pl.multiple_of(ci * HG_CHUNK, HG_CHUNK), HG_CHUNK)
        cols = [slice(h * dh, (h + 1) * dh) for h in range(nh)]
        outs, st_new = _hgrn_chunk([q_ref[0, sl, cs].astype(F32) for cs in cols],
                                   [k_ref[0, sl, cs].astype(F32) for cs in cols],
                                   [v_ref[0, sl, cs].astype(F32) for cs in cols], [lf_ref[0, sl, cs] for cs in cols],
                                   [st_ref[h] for h in range(nh)], tri)
        for h in range(nh):
            st_ref[h] = st_new[h]
            o = outs[h]
            o = o * lax.rsqrt(jnp.mean(o * o, axis=-1, keepdims=True) + RMS_EPS)
            o = o * nw * g_ref[0, sl, cols[h]].astype(F32)
            o_ref[0, sl, cols[h]] = o.astype(o_ref.dtype)
        return carry

    lax.fori_loop(0, n_chunks, body, 0)


def hgrn_recurrence(q, k, v, lf, g, norm_w, *, sblk=512, nhs=8):
    bsz, s, hd = q.shape
    dh = hd // HG_HEADS
    blk = pl.BlockSpec((1, sblk, nhs * dh), lambda b, h, t: (b, t, h))
    return pl.pallas_call(
        functools.partial(_hgrn_rec_kernel, n_chunks=sblk // HG_CHUNK, dh=dh),
        grid=(bsz, HG_HEADS // nhs, s // sblk),
        in_specs=[blk, blk, blk, blk, blk, pl.BlockSpec((1, dh), lambda b, h, t: (0, 0))],
        out_specs=blk,
        out_shape=jax.ShapeDtypeStruct((bsz, s, hd), BF16),
        scratch_shapes=[pltpu.VMEM((nhs, dh, dh), F32)],
        compiler_params=_params("parallel", "parallel", "arbitrary"),
        name="hgrn_recurrence",
    )(q, k, v, lf, g, norm_w)


def _mm_ln_kernel(h_ref, w_ref, x_ref, g_ref, b_ref, o_ref, *, alpha):
    y = jnp.dot(h_ref[...], w_ref[...], preferred_element_type=F32)
    o_ref[...] = _layer_norm(alpha * x_ref[...] + y, g_ref[...], b_ref[...])


def mm_ln(h, w, x2, g, b, *, alpha, tm=512):
    m, kdim = h.shape
    d = w.shape[1]
    row = lambda i: (i, 0)
    fixed = lambda i: (0, 0)
    return pl.pallas_call(
        functools.partial(_mm_ln_kernel, alpha=alpha),
        grid=(m // tm,),
        in_specs=[pl.BlockSpec((tm, kdim), row), pl.BlockSpec((kdim, d), fixed),
                  pl.BlockSpec((tm, d), row), pl.BlockSpec((1, d), fixed), pl.BlockSpec((1, d), fixed)],
        out_specs=pl.BlockSpec((tm, d), row),
        out_shape=jax.ShapeDtypeStruct((m, d), F32),
        compiler_params=_params("parallel"),
        name="mm_ln",
    )(h, w, x2, g, b)


def _ffn_ln_kernel(x_ref, wg_ref, wu_ref, wd_ref, g_ref, b_ref, o_ref, *, alpha):
    xb = x_ref[...].astype(BF16)
    a = jnp.dot(xb, wg_ref[...], preferred_element_type=F32)
    u = jnp.dot(xb, wu_ref[...], preferred_element_type=F32)
    y = jnp.dot((_silu(a) * u).astype(BF16), wd_ref[...], preferred_element_type=F32)
    o_ref[...] = _layer_norm(alpha * x_ref[...] + y, g_ref[...], b_ref[...])


def ffn_ln(x2, w_gu, w_down, g, b, *, alpha, tm=512):
    m, d = x2.shape
    f = w_down.shape[0]
    row = lambda i: (i, 0)
    fixed = lambda i: (0, 0)
    once = pl.Buffered(1)
    return pl.pallas_call(
        functools.partial(_ffn_ln_kernel, alpha=alpha),
        grid=(m // tm,),
        in_specs=[pl.BlockSpec((tm, d), row),
                  pl.BlockSpec((d, f), lambda i: (0, 0), pipeline_mode=once),
                  pl.BlockSpec((d, f), lambda i: (0, 1), pipeline_mode=once),
                  pl.BlockSpec((f, d), fixed, pipeline_mode=once),
                  pl.BlockSpec((1, d), fixed), pl.BlockSpec((1, d), fixed)],
        out_specs=pl.BlockSpec((tm, d), row),
        out_shape=jax.ShapeDtypeStruct((m, d), F32),
        compiler_params=_params("parallel"),
        name="ffn_ln",
    )(x2, w_gu, w_gu, w_down, g, b)


def _router_kernel(x_ref, wh_ref, wl_ref, o_ref):
    x = x_ref[...]
    xh = x.astype(BF16)
    xl = (x - xh.astype(F32)).astype(BF16)
    wh, wl = wh_ref[...], wl_ref[...]
    logits = (jnp.dot(xh, wh, preferred_element_type=F32)
              + (jnp.dot(xl, wh, preferred_element_type=F32) + jnp.dot(xh, wl, preferred_element_type=F32)))
    lane = lax.broadcasted_iota(jnp.int32, logits.shape, 1)
    logits = jnp.where(lane < N_EXPERTS, logits, -jnp.inf)
    m1 = jnp.max(logits, axis=-1, keepdims=True)
    i1 = jnp.min(jnp.where(logits == m1, lane, LANES), axis=-1, keepdims=True)
    rest = jnp.where(lane == i1, -jnp.inf, logits)
    m2 = jnp.max(rest, axis=-1, keepdims=True)
    i2 = jnp.min(jnp.where(rest == m2, lane, LANES), axis=-1, keepdims=True)
    e2 = jnp.exp(m2 - m1)
    den = 1.0 + e2
    w1 = 1.0 / den
    w2 = e2 / den
    o_ref[...] = jnp.where(lane == 0, i1.astype(F32),
                           jnp.where(lane == 1, i2.astype(F32),
                                     jnp.where(lane == 2, w1, jnp.where(lane == 3, w2, 0.0))))


def router(x2, w_hi, w_lo, *, tm=1024):
    m, d = x2.shape
    row = lambda i: (i, 0)
    fixed = lambda i: (0, 0)
    return pl.pallas_call(
        _router_kernel,
        grid=(m // tm,),
        in_specs=[pl.BlockSpec((tm, d), row), pl.BlockSpec((d, LANES), fixed), pl.BlockSpec((d, LANES), fixed)],
        out_specs=pl.BlockSpec((tm, LANES), row),
        out_shape=jax.ShapeDtypeStruct((m, LANES), F32),
        compiler_params=_params("parallel"),
        name="moe_router",
    )(x2, w_hi, w_lo)


def _row_copy(src_hbm, dst_vmem, sem, src_row, dst_row):
    return pltpu.make_async_copy(src_hbm.at[pl.ds(src_row, 1)], dst_vmem.at[pl.ds(dst_row, 1)], sem)


def _gather_start(idx_ref, base, src_hbm, dst_vmem, sem, n_rows):
    for r in range(n_rows):
        _row_copy(src_hbm, dst_vmem, sem, idx_ref[base + r], r).start(priority=r % 2)


def _gather_wait(src_hbm, dst_vmem, sem, n_rows):
    pltpu.make_async_copy(src_hbm.at[pl.ds(0, n_rows)], dst_vmem, sem).wait()


def _gather_kernel(idx_ref, src_hbm, o_ref, sem, *, rows):
    _gather_start(idx_ref, pl.program_id(0) * rows, src_hbm, o_ref, sem, rows)
    _gather_wait(src_hbm, o_ref, sem, rows)


def gather_rows(src, idx, *, rows=1024):
    n = idx.shape[0]
    d = src.shape[1]
    return pl.pallas_call(
        functools.partial(_gather_kernel, rows=rows),
        grid_spec=pltpu.PrefetchScalarGridSpec(
            num_scalar_prefetch=1,
            grid=(n // rows,),
            in_specs=[pl.BlockSpec(memory_space=pl.ANY)],
            out_specs=pl.BlockSpec((rows, d), lambda i, idx: (i, 0)),
            scratch_shapes=[pltpu.SemaphoreType.DMA(())]),
        out_shape=jax.ShapeDtypeStruct((n, d), src.dtype),
        compiler_params=_params("arbitrary"),
        name="moe_gather",
    )(idx, src)


def _experts_kernel(te_ref, nu_ref, xs_ref, wg_ref, wu_ref, wd_ref, o_ref, acc_ref):
    r, c = pl.program_id(0), pl.program_id(1)

    @pl.when(r < nu_ref[0])
    def _():
        xb = xs_ref[...].astype(BF16)
        a = jnp.dot(xb, wg_ref[0], preferred_element_type=F32)
        u = jnp.dot(xb, wu_ref[0], preferred_element_type=F32)
        part = jnp.dot((_silu(a) * u).astype(BF16), wd_ref[0], preferred_element_type=F32)

        @pl.when(c == 0)
        def _():
            acc_ref[...] = part

        @pl.when(c > 0)
        def _():
            acc_ref[...] += part

        @pl.when(c == pl.num_programs(1) - 1)
        def _():
            o_ref[...] = acc_ref[...]

    @pl.when(jnp.logical_and(r >= nu_ref[0], c == pl.num_programs(1) - 1))
    def _():
        o_ref[...] = jnp.zeros_like(o_ref)


def experts(xs, w_gu, w_down, tile_expert, n_used, *, tm, th=1792):
    n, d = xs.shape
    f = w_down.shape[1]
    nc = f // th
    return pl.pallas_call(
        _experts_kernel,
        grid_spec=pltpu.PrefetchScalarGridSpec(
            num_scalar_prefetch=2,
            grid=(n // tm, nc),
            in_specs=[pl.BlockSpec((tm, d), lambda r, c, te, nu: (r, 0)),
                      pl.BlockSpec((1, d, th), lambda r, c, te, nu: (te[r], 0, c)),
                      pl.BlockSpec((1, d, th), lambda r, c, te, nu: (te[r], 0, nc + c)),
                      pl.BlockSpec((1, th, d), lambda r, c, te, nu: (te[r], c, 0))],
            out_specs=pl.BlockSpec((tm, d), lambda r, c, te, nu: (r, 0)),
            scratch_shapes=[pltpu.VMEM((tm, d), F32)]),
        out_shape=jax.ShapeDtypeStruct((n, d), F32),
        compiler_params=_params("arbitrary", "arbitrary"),
        name="moe_experts",
    )(tile_expert, n_used, xs, w_gu, w_gu, w_down)


def _combine_ln_kernel(s1_ref, s2_ref, ys_hbm, x_ref, rw_ref, g_ref, b_ref, o_ref, a_buf, b_buf, sem, *,
                       alpha, rows):
    base = pl.program_id(0) * rows
    _gather_start(s1_ref, base, ys_hbm, a_buf, sem.at[0], rows)
    _gather_start(s2_ref, base, ys_hbm, b_buf, sem.at[1], rows)
    _gather_wait(ys_hbm, a_buf, sem.at[0], rows)
    _gather_wait(ys_hbm, b_buf, sem.at[1], rows)
    rw = rw_ref[...]
    y = rw[:, 2:3] * a_buf[...] + rw[:, 3:4] * b_buf[...]
    o_ref[...] = _layer_norm(alpha * x_ref[...] + y, g_ref[...], b_ref[...])


def combine_ln(ys, slot1, slot2, x2, rw, g, b, *, alpha, rows=512):
    m, d = x2.shape
    row = lambda i, s1, s2: (i, 0)
    fixed = lambda i, s1, s2: (0, 0)
    return pl.pallas_call(
        functools.partial(_combine_ln_kernel, alpha=alpha, rows=rows),
        grid_spec=pltpu.PrefetchScalarGridSpec(
            num_scalar_prefetch=2,
            grid=(m // rows,),
            in_specs=[pl.BlockSpec(memory_space=pl.ANY),
                      pl.BlockSpec((rows, d), row), pl.BlockSpec((rows, LANES), row),
                      pl.BlockSpec((1, d), fixed), pl.BlockSpec((1, d), fixed)],
            out_specs=pl.BlockSpec((rows, d), row),
            scratch_shapes=[pltpu.VMEM((rows, d), F32), pltpu.VMEM((rows, d), F32),
                            pltpu.SemaphoreType.DMA((2,))]),
        out_shape=jax.ShapeDtypeStruct((m, d), F32),
        compiler_params=_params("arbitrary"),
        name="moe_combine_ln",
    )(slot1, slot2, ys, x2, rw, g, b)


def moe_ln(x2, w_r_hi, w_r_lo, w_gu, w_down, g, b, *, alpha, tm=512):
    t = x2.shape[0]
    rw = router(x2, w_r_hi, w_r_lo)
    e_idx = rw[:, :TOP_K].astype(jnp.int32)
    flat_e = e_idx.reshape(-1)
    onehot = (flat_e[:, None] == jnp.arange(N_EXPERTS, dtype=jnp.int32)[None, :]).astype(jnp.int32)
    rank = jnp.take_along_axis(jnp.cumsum(onehot, axis=0) - onehot, flat_e[:, None], axis=1)[:, 0]
    counts = jnp.sum(onehot, axis=0)
    padded = ((counts + tm - 1) // tm) * tm
    ends = jnp.cumsum(padded)
    starts = ends - padded
    slot = starts[flat_e] + rank
    n_tiles = (TOP_K * t) // tm + N_EXPERTS
    n_slots = n_tiles * tm
    token = jnp.arange(TOP_K * t, dtype=jnp.int32) // TOP_K
    token_of_slot = jnp.zeros((n_slots,), jnp.int32).at[slot].set(token)
    tile_start = jnp.arange(n_tiles, dtype=jnp.int32) * tm
    tile_expert = jnp.minimum(jnp.sum((tile_start[:, None] >= ends[None, :]).astype(jnp.int32), axis=1),
                              N_EXPERTS - 1).astype(jnp.int32)
    n_used = (ends[-1:] // tm).astype(jnp.int32)

    xs = gather_rows(x2, token_of_slot)
    ys = experts(xs, w_gu, w_down, tile_expert, n_used, tm=tm)
    slot2d = slot.reshape(t, TOP_K).astype(jnp.int32)
    return combine_ln(ys, slot2d[:, 0], slot2d[:, 1], x2, rw, g, b, alpha=alpha)


def _mm_kernel(x_ref, w_ref, bias_ref, o_ref, *, kind, scale):
    acc = jnp.dot(x_ref[...].astype(BF16), w_ref[...], preferred_element_type=F32)
    if kind == "scale":
        acc = acc * scale
    elif kind == "sigmoid":
        acc = jax.nn.sigmoid(acc)
    elif kind == "silu":
        acc = _silu(acc)
    elif kind == "log_sigmoid_bias":
        acc = _log_sigmoid(acc + bias_ref[...])
    o_ref[...] = acc.astype(o_ref.dtype)


def mm(x2, w, *, kind="none", scale=1.0, bias=None, out_dtype=BF16, tm=1024, tn=None):
    m, d = x2.shape
    n = w.shape[1]
    tn = n if tn is None else tn
    if bias is None:
        bias = jnp.zeros((1, n), F32)
    return pl.pallas_call(
        functools.partial(_mm_kernel, kind=kind, scale=scale),
        grid=(m // tm, n // tn),
        in_specs=[pl.BlockSpec((tm, d), lambda i, j: (i, 0)), pl.BlockSpec((d, tn), lambda i, j: (0, j)),
                  pl.BlockSpec((1, tn), lambda i, j: (0, j))],
        out_specs=pl.BlockSpec((tm, tn), lambda i, j: (i, j)),
        out_shape=jax.ShapeDtypeStruct((m, n), out_dtype),
        compiler_params=_params("parallel", "arbitrary"),
        name="mm_" + kind,
    )(x2, w, bias)


def _seq_cumsum_kernel(x_ref, o_ref, *, blk):
    tri = _tri(blk)
    n = x_ref.shape[1] // blk

    def body(i, carry):
        sl = pl.ds(pl.multiple_of(i * blk, blk), blk)
        c = _cumsum_rows(x_ref[0, sl, :], tri) + carry
        o_ref[0, sl, :] = c
        return c[blk - 1:blk]

    lax.fori_loop(0, n, body, jnp.zeros((1, x_ref.shape[2]), F32))


def seq_cumsum(x, *, blk=256):
    bsz, s, n = x.shape
    spec = pl.BlockSpec((1, s, n), lambda b: (b, 0, 0))
    return pl.pallas_call(
        functools.partial(_seq_cumsum_kernel, blk=blk),
        grid=(bsz,), in_specs=[spec], out_specs=spec,
        out_shape=jax.ShapeDtypeStruct(x.shape, F32),
        compiler_params=_params("parallel"),
        name="seq_cumsum",
    )(x)


def _split3_f32(c):
    return tuple(t.astype(F32) for t in _split3(c))


def _aug_cols(c3, col, width, first):
    hi, mid, lo = (t[:, col:col + 1] for t in c3)
    lane = lax.broadcasted_iota(jnp.int32, (hi.shape[0], width), 1)
    fill = jnp.where(lane < 6, 1.0 if first == 0 else -1.0, 0.0)
    return jnp.where(lane == first, hi, jnp.where(lane == first + 1, mid, jnp.where(lane == first + 2, lo, fill)))


def _fox_kernel(q_ref, cq_ref, ka_ref, vt_ref, gate_ref, o_ref, acc_ref, qt_ref, *, tq, tk):
    qi = pl.program_id(2)
    nhs, dh = acc_ref.shape[0], acc_ref.shape[1]
    rowi = lax.broadcasted_iota(jnp.int32, (tk, tq), 0)
    coli = lax.broadcasted_iota(jnp.int32, (tk, tq), 1)
    acc_ref[...] = jnp.zeros_like(acc_ref)
    c3 = _split3_f32(cq_ref[0, 0] * LOG2E)
    for j in range(nhs):
        qa = jnp.concatenate([q_ref[0, :, j * dh:(j + 1) * dh].astype(F32), _aug_cols(c3, j, dh, 3)], axis=1)
        qt_ref[j] = qa.astype(BF16)

    def tile(ki, stats, masked):
        ks = pl.ds(pl.multiple_of(ki * tk, tk), tk)
        heads = range(len(stats))
        nt = (((1,), (1,)), ((), ()))
        ss = [lax.dot_general(ka_ref[0, j, ks, :], qt_ref[j], nt, preferred_element_type=F32) for j in heads]
        if masked:
            ss = [jnp.where(rowi <= coli, s, NEG_BIG) for s in ss]
        new, alphas, ps = [], [], []
        for j in heads:
            m_old, l_old = stats[j]
            m_new = jnp.maximum(m_old, jnp.max(ss[j], axis=0, keepdims=True))
            alpha = jnp.exp2(m_old - m_new)
            p = jnp.exp2(ss[j] - m_new)
            new.append((m_new, alpha * l_old + jnp.sum(p, axis=0, keepdims=True)))
            alphas.append(alpha)
            ps.append(p.astype(BF16))
        pvs = [jnp.dot(vt_ref[0, j, :, ks], ps[j], preferred_element_type=F32) for j in heads]
        for j in heads:
            acc_ref[j] = alphas[j] * acc_ref[j] + pvs[j]
        return tuple(new)

    init = tuple((jnp.full((1, tq), NEG_BIG, F32), jnp.zeros((1, tq), F32)) for _ in range(nhs))
    stats = lax.fori_loop(0, qi, lambda ki, st: tile(ki, st, False), init)
    stats = tile(qi, stats, True)
    ot = jnp.concatenate([acc_ref[j] / stats[j][1] for j in range(nhs)], axis=0)
    o_ref[0] = (ot.T * gate_ref[0].astype(F32)).astype(o_ref.dtype)


def fox_attention(q, cq, ka, vt, gate, *, tq=256, nhs=8):
    bsz, nh, dh, s = vt.shape
    aug = ka.shape[3]
    hp = nh // nhs
    row = pl.BlockSpec((1, tq, nhs * dh), lambda b, h, t: (b, t, h))
    return pl.pallas_call(
        functools.partial(_fox_kernel, tq=tq, tk=tq),
        grid=(bsz, hp, s // tq),
        in_specs=[row,
                  pl.BlockSpec((1, 1, tq, nhs), lambda b, h, t: (b, h, t, 0)),
                  pl.BlockSpec((1, nhs, s, aug), lambda b, h, t: (b, h, 0, 0)),
                  pl.BlockSpec((1, nhs, dh, s), lambda b, h, t: (b, h, 0, 0)),
                  row],
        out_specs=row,
        out_shape=jax.ShapeDtypeStruct((bsz, s, nh * dh), BF16),
        scratch_shapes=[pltpu.VMEM((nhs, dh, tq), F32), pltpu.VMEM((nhs, tq, aug), BF16)],
        compiler_params=_params("parallel", "parallel", "arbitrary"),
        name="fox_attention",
    )(q, cq, ka, vt, gate)


def _fox_prep_kernel(k_ref, v_ref, c_ref, ka_ref, vt_ref):
    nh, dh = vt_ref.shape[1], vt_ref.shape[2]
    c3 = _split3_f32(c_ref[0] * LOG2E)
    for h in range(nh):
        aug = _aug_cols(c3, h, ka_ref.shape[3] - dh, 0)
        ka_ref[0, h] = jnp.concatenate([k_ref[0, :, h * dh:(h + 1) * dh].astype(F32), aug], axis=1).astype(BF16)
    per = LANES // dh
    for g in range(nh // per):
        vt2 = v_ref[0, :, g * LANES:(g + 1) * LANES].astype(F32).T
        for j in range(per):
            vt_ref[0, g * per + j] = vt2[j * dh:(j + 1) * dh].astype(BF16)


def fox_prep_kv(k_sh, v_sh, c, *, ts=256):
    bsz, s, hd = k_sh.shape
    nh, dh = FOX_HEADS, FOX_HEAD_DIM
    row = pl.BlockSpec((1, ts, hd), lambda b, t: (b, t, 0))
    return pl.pallas_call(
        _fox_prep_kernel,
        grid=(bsz, s // ts),
        in_specs=[row, row, pl.BlockSpec((1, ts, LANES), lambda b, t: (b, t, 0))],
        out_specs=[pl.BlockSpec((1, nh, ts, LANES), lambda b, t: (b, 0, t, 0)),
                   pl.BlockSpec((1, nh, dh, ts), lambda b, t: (b, 0, 0, t))],
        out_shape=[jax.ShapeDtypeStruct((bsz, nh, s, LANES), BF16),
                   jax.ShapeDtypeStruct((bsz, nh, dh, s), BF16)],
        compiler_params=_params("parallel", "parallel"),
        name="fox_prep_kv",
    )(k_sh, v_sh, c)


def kernel(x, hg_w_in, hg_lb_logits, hg_norm_w, hg_w_out, kv_w, kv_fgate_b, fox_w_qg, fox_w_out, ln_mix_g, ln_mix_b, ln_ffn_g, ln_ffn_b, ffn_w_gu, ffn_w_down, moe_w_router, moe_w_gu, moe_w_down):
    bsz, s, d = x.shape
    depth = ln_mix_g.shape[0]
    n_a = hg_w_in.shape[0]
    alpha = (2 * depth) ** 0.25
    fox_dim = FOX_HEADS * FOX_HEAD_DIM
    t = bsz * s

    p = jax.nn.softmax(hg_lb_logits.astype(F32), axis=0)
    lbs = jnp.cumsum(p, axis=0) - p[0]

    def vec(a):
        return a.reshape(1, -1).astype(F32)

    x2 = x.reshape(t, d)
    ka = vt = cq = None
    for layer in range(depth):
        if layer < n_a:
            w_in = hg_w_in[layer].astype(BF16)
            n = w_in.shape[1] // 4
            q = mm(x2, w_in[:, :n], kind="silu")
            lf, kk = hgrn_gate(x2, w_in[:, n:2 * n], vec(lbs[layer]))
            v = mm(x2, w_in[:, 2 * n:3 * n])
            g = mm(x2, w_in[:, 3 * n:], kind="silu")
            sh = (bsz, s, -1)
            o = hgrn_recurrence(q.reshape(sh), kk.reshape(sh), v.reshape(sh), lf.reshape(sh), g.reshape(sh),
                                vec(hg_norm_w[layer]))
            x2 = mm_ln(o.reshape(t, -1), hg_w_out[layer].astype(BF16), x2, vec(ln_mix_g[layer]),
                       vec(ln_mix_b[layer]), alpha=alpha)
        else:
            bi = layer - n_a
            w_qg = fox_w_qg[bi].astype(BF16)
            q = mm(x2, w_qg[:, :fox_dim], kind="scale", scale=FOX_HEAD_DIM ** -0.5 * LOG2E)
            gate = mm(x2, w_qg[:, fox_dim:], kind="sigmoid")
            sh = (bsz, s, fox_dim)
            o = fox_attention(q.reshape(sh), cq, ka, vt, gate.reshape(sh), nhs=FOX_HEADS_PER_STEP)
            x2 = mm_ln(o.reshape(t, fox_dim), fox_w_out[bi].astype(BF16), x2, vec(ln_mix_g[layer]),
                       vec(ln_mix_b[layer]), alpha=alpha)
        if layer % 2 == 0:
            x2 = ffn_ln(x2, ffn_w_gu[layer // 2].astype(BF16), ffn_w_down[layer // 2].astype(BF16),
                        vec(ln_ffn_g[layer]), vec(ln_ffn_b[layer]), alpha=alpha)
        else:
            w_r = jnp.pad(moe_w_router[layer // 2].astype(F32), ((0, 0), (0, LANES - N_EXPERTS)))
            w_r_hi = w_r.astype(BF16)
            w_r_lo = (w_r - w_r_hi.astype(F32)).astype(BF16)
            x2 = moe_ln(x2, w_r_hi, w_r_lo, moe_w_gu[layer // 2].astype(BF16), moe_w_down[layer // 2].astype(BF16),
                        vec(ln_ffn_g[layer]), vec(ln_ffn_b[layer]), alpha=alpha)
        if layer == n_a - 1:
            w_kv = kv_w.astype(BF16)
            k_sh = mm(x2, w_kv[:, :fox_dim]).reshape(bsz, s, fox_dim)
            v_sh = mm(x2, w_kv[:, fox_dim:2 * fox_dim]).reshape(bsz, s, fox_dim)
            w_f = jnp.pad(w_kv[:, 2 * fox_dim:], ((0, 0), (0, LANES - FOX_HEADS)))
            b_f = jnp.pad(kv_fgate_b.astype(F32), (0, LANES - FOX_HEADS)).reshape(1, LANES)
            log_f = mm(x2, w_f, kind="log_sigmoid_bias", bias=b_f, out_dtype=F32)
            c = seq_cumsum(log_f.reshape(bsz, s, LANES))
            ka, vt = fox_prep_kv(k_sh, v_sh, c)
            cq = c[:, :, :FOX_HEADS].reshape(bsz, s, FOX_HEADS // FOX_HEADS_PER_STEP, FOX_HEADS_PER_STEP)
            cq = cq.transpose(0, 2, 1, 3)
    return x2.reshape(bsz, s, d)
```
